```python
import math
import jax, jax.numpy as jnp
from jax import lax
import numpy as np

D_MODEL = 1024
BATCH = 32
SEQ = 2048
DEPTH = 4
DEC_BATCH = 8
DEC_SEQ = 4096
PAST_LEN = 128

F32 = jnp.float32
EPS = 1e-6
D_PLE = 256
D_FF = 2816
A_GROUPS = ((128, 1), (512, 4), (2048, 16))
A_HEADS_PER_GROUP = 4
A_HEAD_DIM = 64
A_HEADS = 12
A_WIDTH = 768
A_OUT = 256
A_BLOCK = 64
N_BUCKETS = 32
MAX_DISTANCE = 1024
B_HEADS = 4
B_HEAD_DIM = 128
B_WIDTH = 512
RET_CHUNK = 128
ROPE_BASE = 10000.0
C_HEADS = 4
C_KEY_DIM = 128
C_VAL_DIM = 128
C_WIDTH = 512
GLA_CHUNK = 64
SPLITS = (A_WIDTH, A_WIDTH, A_WIDTH, B_WIDTH, B_WIDTH, B_WIDTH, B_WIDTH, C_WIDTH, C_WIDTH, C_WIDTH, C_WIDTH, C_WIDTH)
N_IN = 3 * A_WIDTH + 4 * B_WIDTH + 5 * C_WIDTH

kernel_name = 'hybrid_bidir_dilated_retention_hgrn2_encoder'


def _rmsnorm(x, g):
    x32 = x.astype(F32)
    y = x32 * lax.rsqrt(jnp.mean(x32 * x32, axis=-1, keepdims=True) + EPS)
    return (y * g.astype(F32)).astype(x.dtype)


def _swiglu(u, wg, wu, wd):
    return (jax.nn.silu(u @ wg) * (u @ wu)) @ wd


def _t5_bucket(rel):
    half = N_BUCKETS // 2
    max_exact = half // 2
    ret = jnp.where(rel > 0, half, 0)
    n = jnp.abs(rel)
    nf = jnp.maximum(n, 1).astype(F32)
    large = max_exact + (jnp.log(nf / max_exact) / math.log(MAX_DISTANCE / max_exact) * (half - max_exact)).astype(jnp.int32)
    large = jnp.minimum(large, half - 1)
    return ret + jnp.where(n < max_exact, n, large)


def _dilated_band_attention(q, k, v, bias_table, dilation, n_side):
    Bn, S, H, hd = q.shape
    L = S // dilation
    C = A_BLOCK
    nC = -(-L // C)
    Lp = nC * C

    def to_sub(t):
        t = t.reshape(Bn, L, dilation, H, hd).transpose(0, 2, 1, 3, 4)
        t = jnp.pad(t, ((0, 0), (0, 0), (0, Lp - L), (0, 0), (0, 0)))
        return t.reshape(Bn, dilation, nC, C, H, hd)

    def band(t):
        tp = jnp.pad(t, ((0, 0), (0, 0), (1, 1), (0, 0), (0, 0), (0, 0)))
        return jnp.concatenate([tp[:, :, :-2], tp[:, :, 1:-1], tp[:, :, 2:]], axis=3)

    qs = to_sub(q)
    kb, vb = band(to_sub(k)), band(to_sub(v))
    qi = np.arange(C)[:, None]
    kj = np.arange(3 * C)[None, :]
    off = kj - C - qi
    bias = bias_table[_t5_bucket(jnp.asarray(off * dilation, dtype=jnp.int32))]
    bias = jnp.transpose(bias, (2, 0, 1)).astype(F32)
    kpos = (np.arange(nC)[:, None, None] - 1) * C + kj[None]
    valid = (np.abs(off) <= n_side)[None] & (kpos >= 0) & (kpos < L)
    s = jnp.einsum('brnqhd,brnkhd->brnhqk', qs, kb).astype(F32) * (hd ** -0.5) + bias
    s = jnp.where(valid[:, None], s, -1e30)
    m = jnp.max(s, axis=-1, keepdims=True)
    e = jnp.exp(s - m)
    den = jnp.sum(e, axis=-1, keepdims=True)
    o = jnp.einsum('brnhqk,brnkhd->brnqhd', (e / den).astype(v.dtype), vb)
    lse = jnp.swapaxes((m + jnp.log(den))[..., 0], 3, 4)

    def from_sub(t):
        t = t.reshape((Bn, dilation, Lp) + t.shape[4:])[:, :, :L]
        t = jnp.swapaxes(t, 1, 2)
        return t.reshape((Bn, S) + t.shape[3:])

    return from_sub(o), from_sub(lse)


def _mixer_a(q, k, v, rel_bias):
    Bn, S, _ = q.shape
    shp = (Bn, S, A_HEADS, A_HEAD_DIM)
    q, k, v = q.reshape(shp), k.reshape(shp), v.reshape(shp)
    outs, lses = [], []
    for g, (window, dil) in enumerate(A_GROUPS):
        sl = slice(g * A_HEADS_PER_GROUP, (g + 1) * A_HEADS_PER_GROUP)
        o, lse = _dilated_band_attention(q[:, :, sl], k[:, :, sl], v[:, :, sl], rel_bias[:, sl], dil, (window // 2) // dil)
        outs.append(o)
        lses.append(lse)
    w = jax.nn.softmax(jnp.stack(lses), axis=0)[..., None]
    out = jnp.sum(w * jnp.stack(outs).astype(F32), axis=0)
    return out.reshape(Bn, S, A_OUT).astype(q.dtype)


def _rotary(x, pos):
    half = x.shape[-1] // 2
    inv = ROPE_BASE ** (-jnp.arange(half, dtype=F32) / half)
    ang = pos[:, None] * inv[None]
    cos, sin = jnp.cos(ang)[None, :, None, :], jnp.sin(ang)[None, :, None, :]
    x1, x2 = x[..., :half], x[..., half:]
    return jnp.concatenate([x1 * cos - x2 * sin, x1 * sin + x2 * cos], axis=-1).astype(x.dtype)


def _retention_dir(q, k, v, log_gamma):
    Bn, S, H, dk = q.shape
    dv = v.shape[-1]
    C = RET_CHUNK
    n = S // C
    q = q.reshape(Bn, n, C, H, dk)
    k = k.reshape(Bn, n, C, H, dk)
    v = v.reshape(Bn, n, C, H, dv)
    idx = jnp.arange(C, dtype=F32)
    rel = idx[:, None] - idx[None, :]
    decay = jnp.where((rel >= 0)[..., None], jnp.exp(jnp.maximum(rel, 0.0)[..., None] * log_gamma), 0.0)
    att = jnp.einsum('bnihd,bnjhd->bnhij', q, k) * jnp.transpose(decay, (2, 0, 1))
    o = jnp.einsum('bnhij,bnjhe->bnihe', att, v)
    k_dec = k * jnp.exp((C - 1 - idx)[:, None] * log_gamma)[:, :, None]
    kv = jnp.einsum('bnjhd,bnjhe->nbhde', k_dec, v)
    g_chunk = jnp.exp(C * log_gamma)[:, None, None]

    def step(state, kv_c):
        return state * g_chunk + kv_c, state

    _, prev = lax.scan(step, jnp.zeros(kv.shape[1:], kv.dtype), kv)
    q_dec = q * jnp.exp((idx + 1)[:, None] * log_gamma)[:, :, None]
    o = o + jnp.einsum('bnihd,nbhde->bnihe', q_dec, prev)
    return o.reshape(Bn, S, H, dv)


def _mixer_b(q, k, v, g, gain):
    Bn, S, _ = q.shape
    shp = (Bn, S, B_HEADS, B_HEAD_DIM)
    q, k, v = q.reshape(shp), k.reshape(shp), v.reshape(shp)
    pos = jnp.arange(S, dtype=F32)
    q = _rotary(q, pos)
    k = _rotary(k, pos) * (B_HEAD_DIM ** -0.5)
    hh = jnp.arange(B_HEADS, dtype=F32)
    lg_fwd = jnp.log1p(-jnp.exp2(-5.0 - hh))
    lg_bwd = jnp.log1p(-jnp.exp2(-5.5 - hh))
    o = _retention_dir(q, k, v, lg_fwd) + _retention_dir(q[:, ::-1], k[:, ::-1], v[:, ::-1], lg_bwd)[:, ::-1]
    o = o.astype(F32)
    mu = jnp.mean(o, axis=-1, keepdims=True)
    var = jnp.mean(jnp.square(o - mu), axis=-1, keepdims=True)
    o = ((o - mu) * lax.rsqrt(var + EPS)).reshape(Bn, S, B_WIDTH) * gain.astype(F32)
    return (jax.nn.silu(g.astype(F32)) * o).astype(g.dtype)


def _gla_dir(q, k, v, logf):
    Bn, S, H, dk = q.shape
    dv = v.shape[-1]
    C = GLA_CHUNK
    n = S // C
    q = q.reshape(Bn, n, C, H, dk)
    k = k.reshape(Bn, n, C, H, dk)
    v = v.reshape(Bn, n, C, H, dv)
    b = jnp.cumsum(logf.reshape(Bn, n, C, H, dk), axis=2)
    b_ref = b[:, :, C // 2 - 1:C // 2]
    att = jnp.einsum('bnihd,bnjhd->bnhij', q * jnp.exp(b - b_ref), k * jnp.exp(b_ref - b))
    tri = jnp.tril(jnp.ones((C, C), dtype=bool))
    att = jnp.where(tri, att, 0.0)
    o = jnp.einsum('bnhij,bnjhe->bnihe', att, v)
    b_last = b[:, :, -1:]
    kv = jnp.einsum('bnjhd,bnjhe->nbhde', k * jnp.exp(b_last - b), v)
    g_chunk = jnp.moveaxis(jnp.exp(b_last[:, :, 0]), 1, 0)

    def step(state, inp):
        kv_c, g_c = inp
        return state * g_c[..., None] + kv_c, state

    _, prev = lax.scan(step, jnp.zeros(kv.shape[1:], kv.dtype), (kv, g_chunk))
    o = o + jnp.einsum('bnihd,nbhde->bnihe', q * jnp.exp(b), prev)
    return o.reshape(Bn, S, H, dv)


def _mixer_c(q, zf, zb, v, g, lb, gain):
    Bn, S, _ = q.shape
    kshp = (Bn, S, C_HEADS, C_KEY_DIM)
    q = jax.nn.silu(q).reshape(kshp)
    v = v.reshape(Bn, S, C_HEADS, C_VAL_DIM)
    log_lb, log_1mlb = jnp.log(lb), jnp.log1p(-lb)

    def gate(z):
        z = z.astype(F32)
        logf = jnp.logaddexp(log_lb, log_1mlb + jax.nn.log_sigmoid(z))
        key = jnp.exp(log_1mlb + jax.nn.log_sigmoid(-z))
        return logf.reshape(kshp), key.reshape(kshp)

    logf_f, k_f = gate(zf)
    logf_b, k_b = gate(zb)
    o = _gla_dir(q, k_f, v, logf_f) + _gla_dir(q[:, ::-1], k_b[:, ::-1], v[:, ::-1], logf_b[:, ::-1])[:, ::-1]
    o = o.astype(F32)
    o = o * lax.rsqrt(jnp.mean(o * o, axis=-1, keepdims=True) + EPS)
    o = o.reshape(Bn, S, C_WIDTH) * gain.astype(F32)
    return (jax.nn.silu(g.astype(F32)) * o).astype(g.dtype)


def _trunk(x, p, lbs, w):
    h = x
    cuts = [int(c) for c in np.cumsum(SPLITS)[:-1]]
    for l in range(DEPTH):
        h = h + 0.5 * _swiglu(_rmsnorm(h, w['ffn1_norm'][l]), w['ffn1_w_gate'][l], w['ffn1_w_up'][l], w['ffn1_w_down'][l])
        u = _rmsnorm(h, w['mix_norm'][l])
        z = u @ w['w_in'][l]
        qa, ka, va, qb, kb, vb, gb, qc, fcf, fcb, ic, gc = jnp.split(z, cuts, axis=-1)
        a = _mixer_a(qa, ka, va, w['rel_bias'])
        b = _mixer_b(qb, kb, vb, gb, w['ret_norm'][l])
        c = _mixer_c(qc, fcf, fcb, ic, gc, lbs[l], w['hgrn_norm'][l])
        ga, gbr, gcr = jnp.split(jax.nn.sigmoid(u @ w['w_merge_gate'][l]), 3, axis=-1)
        m = ga * (a @ w['w_branch_a'][l]) + gbr * (b @ w['w_branch_b'][l]) + gcr * (c @ w['w_branch_c'][l])
        h = h + m @ w['w_out'][l]
        h = h + 0.5 * _swiglu(_rmsnorm(h, w['ffn2_norm'][l]), w['ffn2_w_gate'][l], w['ffn2_w_up'][l], w['ffn2_w_down'][l])
        h = h + jax.nn.sigmoid(_rmsnorm(h, w['ple_norm'][l]) @ w['w_ple_gate'][l]) * (p[l] @ w['w_ple_proj'][l])
    return _rmsnorm(h, w['final_norm'])


def setup_inputs(seed: int = 0) -> dict:
    key = jax.random.key(seed)
    ks = iter(jax.random.split(key, 32))
    D, F = D_MODEL, D_FF

    def nrm(shape, scale):
        return jax.random.normal(next(ks), shape, F32) * scale

    def gain(shape):
        return 1.0 + 0.05 * jax.random.normal(next(ks), shape, F32)

    return {
        'x_prompt': nrm((BATCH, SEQ, D), 1.0),
        'x_sample': nrm((DEC_BATCH, DEC_SEQ, D), 1.0),
        'p_prompt': nrm((DEPTH, BATCH, SEQ, D_PLE), 1.0),
        'p_sample': nrm((DEPTH, DEC_BATCH, DEC_SEQ, D_PLE), 1.0),
        'ffn1_norm': gain((DEPTH, D)),
        'ffn1_w_gate': nrm((DEPTH, D, F), D ** -0.5),
        'ffn1_w_up': nrm((DEPTH, D, F), D ** -0.5),
        'ffn1_w_down': nrm((DEPTH, F, D), F ** -0.5),
        'mix_norm': gain((DEPTH, D)),
        'w_in': nrm((DEPTH, D, N_IN), D ** -0.5),
        'rel_bias': nrm((N_BUCKETS, A_HEADS), 0.5),
        'ret_norm': gain((DEPTH, B_WIDTH)),
        'hgrn_lower_bound': nrm((DEPTH, C_WIDTH), 1.0),
        'hgrn_norm': gain((DEPTH, C_WIDTH)),
        'w_branch_a': nrm((DEPTH, A_OUT, D), A_OUT ** -0.5),
        'w_branch_b': nrm((DEPTH, B_WIDTH, D), B_WIDTH ** -0.5),
        'w_branch_c': nrm((DEPTH, C_WIDTH, D), C_WIDTH ** -0.5),
        'w_merge_gate': nrm((DEPTH, D, 3 * D), D ** -0.5),
        'w_out': nrm((DEPTH, D, D), D ** -0.5),
        'ffn2_norm': gain((DEPTH, D)),
        'ffn2_w_gate': nrm((DEPTH, D, F), D ** -0.5),
        'ffn2_w_up': nrm((DEPTH, D, F), D ** -0.5),
        'ffn2_w_down': nrm((DEPTH, F, D), F ** -0.5),
        'ple_norm': gain((DEPTH, D)),
        'w_ple_gate': nrm((DEPTH, D, D), D ** -0.5),
        'w_ple_proj': nrm((DEPTH, D_PLE, D), D_PLE ** -0.5),
        'final_norm': gain((D,)),
    }


def reference(x_prompt, x_sample, p_prompt, p_sample, ffn1_norm, ffn1_w_gate, ffn1_w_up, ffn1_w_down, mix_norm, w_in, rel_bias, ret_norm, hgrn_lower_bound, hgrn_norm, w_branch_a, w_branch_b, w_branch_c, w_merge_gate, w_out, ffn2_norm, ffn2_w_gate, ffn2_w_up, ffn2_w_down, ple_norm, w_ple_gate, w_ple_proj, final_norm):
    lb = jax.nn.softmax(hgrn_lower_bound.astype(F32), axis=0)
    lb = jnp.cumsum(lb, axis=0)
    lbs = lb - lb[0]
    w = dict(ffn1_norm=ffn1_norm, ffn1_w_gate=ffn1_w_gate, ffn1_w_up=ffn1_w_up, ffn1_w_down=ffn1_w_down,
             mix_norm=mix_norm, w_in=w_in, rel_bias=rel_bias, ret_norm=ret_norm, hgrn_norm=hgrn_norm,
             w_branch_a=w_branch_a, w_branch_b=w_branch_b, w_branch_c=w_branch_c, w_merge_gate=w_merge_gate,
             w_out=w_out, ffn2_norm=ffn2_norm, ffn2_w_gate=ffn2_w_gate, ffn2_w_up=ffn2_w_up,
             ffn2_w_down=ffn2_w_down, ple_norm=ple_norm, w_ple_gate=w_ple_gate, w_ple_proj=w_ple_proj,
             final_norm=final_norm)
    y_prompt = _trunk(x_prompt, p_prompt, lbs, w)
    y_sample = _trunk(x_sample, p_sample, lbs, w)
    return (y_prompt, y_sample)
```

```python
import functools
import math

import jax
import jax.numpy as jnp
import numpy as np
from jax import lax
from jax.experimental import pallas as pl
from jax.experimental.pallas import tpu as pltpu

F32 = jnp.float32
BF16 = jnp.bfloat16

EPS = 1e-6
D_MODEL = 1024
D_PLE = 256
D_FF = 2816
DEPTH = 4
A_GROUPS = ((128, 1), (512, 4), (2048, 16))
A_HEADS_PER_GROUP = 4
A_HEAD_DIM = 64
A_HEADS = 12
A_GROUP_WIDTH = A_HEADS_PER_GROUP * A_HEAD_DIM
A_WIDTH = 768
A_BLOCK = 64
N_BUCKETS = 32
MAX_DISTANCE = 1024
B_HEADS = 4
B_HEAD_DIM = 128
B_WIDTH = 512
RET_CHUNK = 128
ROPE_BASE = 10000.0
C_HEADS = 4
C_HEAD_DIM = 128
C_WIDTH = 512
GLA_CHUNK = 64
N_IN = 3 * A_WIDTH + 4 * B_WIDTH + 5 * C_WIDTH
LANE = 128
COL_B = 3 * A_WIDTH // LANE
COL_C = COL_B + 4 * B_WIDTH // LANE
HEAD_COLS = B_WIDTH // LANE

VMEM_LIMIT_BYTES = 56 * 1024 * 1024
TOKEN_TILE = 512
FFN_F_TILE = 1408
PROJ_N_TILE = 768


def _params(*sem):
    return pltpu.CompilerParams(dimension_semantics=sem, vmem_limit_bytes=VMEM_LIMIT_BYTES)


def _rms_bf16(x, g):
    ms = jnp.mean(x * x, axis=-1, keepdims=True)
    return (x * lax.rsqrt(ms + EPS) * g).astype(BF16)


def _dot(a, b):
    return jnp.dot(a, b, preferred_element_type=F32)


def _dot_nt(a, b):
    return lax.dot_general(a, b, (((1,), (1,)), ((), ())), preferred_element_type=F32)


def _ffn_kernel(x_ref, g_ref, wg_ref, wu_ref, wd_ref, o_ref, u_scr, acc_scr):
    f = pl.program_id(1)

    @pl.when(f == 0)
    def _():
        u_scr[...] = _rms_bf16(x_ref[...], g_ref[...])
        acc_scr[...] = jnp.zeros_like(acc_scr)

    u = u_scr[...]
    a = _dot(u, wg_ref[...])
    b = _dot(u, wu_ref[...])
    hid = (a * jax.nn.sigmoid(a) * b).astype(BF16)
    acc_scr[...] += _dot(hid, wd_ref[...])

    @pl.when(f == pl.num_programs(1) - 1)
    def _():
        o_ref[...] = x_ref[...] + 0.5 * acc_scr[...]


def _ffn(h, g, wg, wu, wd, layer):
    T = h.shape[0]
    tm, tf = TOKEN_TILE, FFN_F_TILE
    return pl.pallas_call(
        _ffn_kernel,
        out_shape=jax.ShapeDtypeStruct((T, D_MODEL), F32),
        grid=(T // tm, D_FF // tf),
        in_specs=[
            pl.BlockSpec((tm, D_MODEL), lambda i, f: (i, 0)),
            pl.BlockSpec((None, 1, D_MODEL), lambda i, f: (layer, 0, 0)),
            pl.BlockSpec((None, D_MODEL, tf), lambda i, f: (layer, 0, f)),
            pl.BlockSpec((None, D_MODEL, tf), lambda i, f: (layer, 0, f)),
            pl.BlockSpec((None, tf, D_MODEL), lambda i, f: (layer, f, 0)),
        ],
        out_specs=pl.BlockSpec((tm, D_MODEL), lambda i, f: (i, 0)),
        scratch_shapes=[pltpu.VMEM((tm, D_MODEL), BF16), pltpu.VMEM((tm, D_MODEL), F32)],
        compiler_params=_params("parallel", "arbitrary"),
        name="ffn",
    )(h, g, wg, wu, wd)


def _proj_kernel(x_ref, g_ref, w_ref, o_ref, u_scr, *, sigmoid):
    @pl.when(pl.program_id(1) == 0)
    def _():
        u_scr[...] = _rms_bf16(x_ref[...], g_ref[...])

    z = _dot(u_scr[...], w_ref[...])
    if sigmoid:
        z = jax.nn.sigmoid(z)
    o_ref[...] = z.astype(o_ref.dtype)


def _rms_proj(h, g, w, layer, sigmoid, name):
    T = h.shape[0]
    n = w.shape[-1]
    tm, tn = TOKEN_TILE, PROJ_N_TILE
    return pl.pallas_call(
        functools.partial(_proj_kernel, sigmoid=sigmoid),
        out_shape=jax.ShapeDtypeStruct((T, n), BF16),
        grid=(T // tm, n // tn),
        in_specs=[
            pl.BlockSpec((tm, D_MODEL), lambda i, j: (i, 0)),
            pl.BlockSpec((None, 1, D_MODEL), lambda i, j: (layer, 0, 0)),
            pl.BlockSpec((None, D_MODEL, tn), lambda i, j: (layer, 0, j)),
        ],
        out_specs=pl.BlockSpec((tm, tn), lambda i, j: (i, j)),
        scratch_shapes=[pltpu.VMEM((tm, D_MODEL), BF16)],
        compiler_params=_params("parallel", "arbitrary"),
        name=name,
    )(h, g, w)


def _t5_bucket_np(rel):
    half = N_BUCKETS // 2
    max_exact = half // 2
    ret = np.where(rel > 0, half, 0)
    n = np.abs(rel)
    nf = np.maximum(n, 1).astype(np.float32)
    large = max_exact + (
        np.log(nf / np.float32(max_exact)) / np.float32(math.log(MAX_DISTANCE / max_exact)) * np.float32(half - max_exact)
    ).astype(np.int32)
    large = np.minimum(large, half - 1)
    return ret + np.where(n < max_exact, n, large)


def _band_attn_kernel(q_ref, k_ref, v_ref, bias_ref, o_ref, lse_ref, *, seq_len, n_side):
    C = A_BLOCK
    H = A_HEADS_PER_GROUP
    hd = A_HEAD_DIM
    W = A_GROUP_WIDTH
    n_blocks = seq_len // C
    scale = hd ** -0.5

    row = lax.broadcasted_iota(jnp.int32, (H * C, 3 * C), 0)
    kj = lax.broadcasted_iota(jnp.int32, (H * C, 3 * C), 1)
    off = kj - C - (row & (C - 1))
    band = jnp.abs(off) <= n_side
    lane_head_q = lax.broadcasted_iota(jnp.int32, (C, W), 1) // hd

    def body(n, carry):
        base = pl.multiple_of(n * C, C)
        qb = q_ref[pl.ds(base, C), :].astype(F32)
        kb = k_ref[pl.ds(base, 3 * C), :]
        vb = v_ref[pl.ds(base, 3 * C), :]
        qexp = jnp.concatenate([jnp.where(lane_head_q == j, qb, 0.0) for j in range(H)], axis=0).astype(BF16)
        s = _dot_nt(qexp, kb) * scale + bias_ref[...]
        kpos = (n - 1) * C + kj
        valid = band & (kpos >= 0) & (kpos < seq_len)
        s = jnp.where(valid, s, -1e30)
        m = jnp.max(s, axis=-1, keepdims=True)
        e = jnp.exp(s - m)
        den = jnp.sum(e, axis=-1, keepdims=True)
        p = (e / den).astype(BF16)
        oall = _dot(p, vb)
        lse = m + jnp.log(den)
        o = jnp.zeros((C, W), F32)
        lse_b = jnp.zeros((C, W), F32)
        for j in range(H):
            sel = lane_head_q == j
            o = jnp.where(sel, oall[j * C:(j + 1) * C, :], o)
            lse_b = jnp.where(sel, lse[j * C:(j + 1) * C, :], lse_b)
        o_ref[pl.ds(base, C), :] = o
        lse_ref[pl.ds(base, C), :] = lse_b
        return carry

    lax.fori_loop(0, n_blocks, body, 0)


def _band_attn(q, k, v, bias, n_side):
    N, L, W = q.shape
    C = A_BLOCK
    kern = functools.partial(_band_attn_kernel, seq_len=L, n_side=n_side)
    return pl.pallas_call(
        kern,
        out_shape=(jax.ShapeDtypeStruct((N, L, W), F32), jax.ShapeDtypeStruct((N, L, W), F32)),
        grid=(N,),
        in_specs=[
            pl.BlockSpec((None, L, W), lambda i: (i, 0, 0)),
            pl.BlockSpec((None, L + 2 * C, W), lambda i: (i, 0, 0)),
            pl.BlockSpec((None, L + 2 * C, W), lambda i: (i, 0, 0)),
            pl.BlockSpec((A_HEADS_PER_GROUP * C, 3 * C), lambda i: (0, 0)),
        ],
        out_specs=(pl.BlockSpec((None, L, W), lambda i: (i, 0, 0)), pl.BlockSpec((None, L, W), lambda i: (i, 0, 0))),
        compiler_params=_params("parallel"),
        name="band_attn",
    )(q, k, v, bias)


def _mixer_a(z3, rel_bias):
    Bn, S, _ = z3.shape
    C = A_BLOCK
    outs, lses = [], []
    qi = np.arange(C)[:, None]
    kj = np.arange(3 * C)[None, :]
    off = kj - C - qi
    for g, (window, dil) in enumerate(A_GROUPS):
        n_side = (window // 2) // dil
        L = S // dil
        assert L % C == 0 and L >= C
        bucket = _t5_bucket_np((off * dil).astype(np.int32))
        tbl = rel_bias[:, g * A_HEADS_PER_GROUP:(g + 1) * A_HEADS_PER_GROUP].astype(F32)
        bias = jnp.transpose(tbl[bucket], (2, 0, 1)).reshape(A_HEADS_PER_GROUP * C, 3 * C)

        def to_sub(col):
            t = z3[:, :, col * A_GROUP_WIDTH:(col + 1) * A_GROUP_WIDTH]
            t = t.reshape(Bn, L, dil, A_GROUP_WIDTH).transpose(0, 2, 1, 3)
            return t.reshape(Bn * dil, L, A_GROUP_WIDTH)

        q = to_sub(g)
        k = jnp.pad(to_sub(3 + g), ((0, 0), (C, C), (0, 0)))
        v = jnp.pad(to_sub(6 + g), ((0, 0), (C, C), (0, 0)))
        o, lse = _band_attn(q, k, v, bias, n_side)

        def from_sub(t):
            t = t.reshape(Bn, dil, L, A_GROUP_WIDTH).transpose(0, 2, 1, 3)
            return t.reshape(Bn * S, A_GROUP_WIDTH)

        outs.append(from_sub(o))
        lses.append(from_sub(lse))
    return outs, lses


def _ret_consts():
    C = RET_CHUNK
    hh = jnp.arange(B_HEADS, dtype=F32)
    lgf = jnp.log1p(-jnp.exp2(-5.0 - hh))[:, None, None]
    lgb = jnp.log1p(-jnp.exp2(-5.5 - hh))[:, None, None]
    i = jnp.arange(C, dtype=F32)[None, :, None]
    j = jnp.arange(C, dtype=F32)[None, None, :]
    rel = i - j
    dfb = (jnp.where(rel >= 0, jnp.exp(jnp.maximum(rel, 0.0) * lgf), 0.0)
           + jnp.where(rel <= 0, jnp.exp(jnp.maximum(-rel, 0.0) * lgb), 0.0))
    ones = jnp.ones((1, 1, C), F32)
    qdf = jnp.exp((i + 1.0) * lgf) * ones
    qdb = jnp.exp((C - i) * lgb) * ones
    kdf = jnp.exp((C - 1.0 - i) * lgf) * ones
    kdb = jnp.exp(i * lgb) * ones
    gcf = jnp.exp(C * lgf) * jnp.ones((1, C, C), F32)
    gcb = jnp.exp(C * lgb) * jnp.ones((1, C, C), F32)
    return jnp.stack([dfb, qdf, qdb, kdf, kdb, gcf, gcb], axis=1)


def _ret_kernel(q_ref, k_ref, v_ref, g_ref, cos_ref, sin_ref, c_ref, gain_ref, o_ref, qs, ks, os_, *, seq_len):
    C = RET_CHUNK
    n_chunks = seq_len // C
    scale = B_HEAD_DIM ** -0.5

    def rot(x, sl):
        return x * cos_ref[sl, :] + pltpu.roll(x, B_HEAD_DIM // 2, 1) * sin_ref[sl, :]

    def fwd(n, st):
        sl = pl.ds(pl.multiple_of(n * C, C), C)
        q = rot(q_ref[sl, :].astype(F32), sl)
        k = rot(k_ref[sl, :].astype(F32), sl) * scale
        v = v_ref[sl, :]
        qs[sl, :] = q
        ks[sl, :] = k
        qb = q.astype(BF16)
        kb = k.astype(BF16)
        att = (_dot_nt(qb, kb) * c_ref[0]).astype(BF16)
        o = _dot(att, v) + _dot_nt((q * c_ref[1]).astype(BF16), st.astype(BF16))
        os_[sl, :] = o
        vt = v.astype(F32).T.astype(BF16)
        kv_t = _dot(vt, (k * c_ref[3]).astype(BF16))
        return st * c_ref[5] + kv_t

    lax.fori_loop(0, n_chunks, fwd, jnp.zeros((C, C), F32))

    def bwd(i, st):
        n = n_chunks - 1 - i
        sl = pl.ds(pl.multiple_of(n * C, C), C)
        q = qs[sl, :]
        k = ks[sl, :]
        v = v_ref[sl, :]
        o = os_[sl, :] + _dot_nt((q * c_ref[2]).astype(BF16), st.astype(BF16))
        mu = jnp.mean(o, axis=-1, keepdims=True)
        d = o - mu
        var = jnp.mean(d * d, axis=-1, keepdims=True)
        y = d * lax.rsqrt(var + EPS) * gain_ref[...]
        g = g_ref[sl, :].astype(F32)
        o_ref[sl, :] = (g * jax.nn.sigmoid(g) * y).astype(o_ref.dtype)
        vt = v.astype(F32).T.astype(BF16)
        kv_t = _dot(vt, (k * c_ref[4]).astype(BF16))
        return st * c_ref[6] + kv_t

    lax.fori_loop(0, n_chunks, bwd, jnp.zeros((C, C), F32))


def _mixer_b(z3, cos2, sin2, consts, gain):
    Bn, S, _ = z3.shape
    C = RET_CHUNK

    def zspec(slab):
        return pl.BlockSpec((None, S, LANE), lambda b, h: (b, 0, COL_B + slab * HEAD_COLS + h))

    return pl.pallas_call(
        functools.partial(_ret_kernel, seq_len=S),
        out_shape=jax.ShapeDtypeStruct((Bn, S, B_WIDTH), BF16),
        grid=(Bn, B_HEADS),
        in_specs=[
            zspec(0), zspec(1), zspec(2), zspec(3),
            pl.BlockSpec((S, LANE), lambda b, h: (0, 0)),
            pl.BlockSpec((S, LANE), lambda b, h: (0, 0)),
            pl.BlockSpec((None, 7, C, C), lambda b, h: (h, 0, 0, 0)),
            pl.BlockSpec((1, LANE), lambda b, h: (0, h)),
        ],
        out_specs=pl.BlockSpec((None, S, LANE), lambda b, h: (b, 0, h)),
        scratch_shapes=[pltpu.VMEM((S, LANE), F32), pltpu.VMEM((S, LANE), F32), pltpu.VMEM((S, LANE), F32)],
        compiler_params=_params("parallel", "arbitrary"),
        name="retention",
    )(z3, z3, z3, z3, cos2, sin2, consts, gain)


def _split3(x):
    hi = x.astype(BF16)
    r1 = x - hi.astype(F32)
    mid = r1.astype(BF16)
    lo = (r1 - mid.astype(F32)).astype(BF16)
    return hi, mid, lo


def _tri_cumsum(tri, x):
    hi, mid, lo = _split3(x)
    return _dot(tri, hi) + _dot(tri, mid) + _dot(tri, lo)


def _gate(z, log_lb, log_1mlb):
    t = jnp.log1p(jnp.exp(-jnp.abs(z)))
    ls_pos = jnp.minimum(z, 0.0) - t
    ls_neg = jnp.minimum(-z, 0.0) - t
    c = log_1mlb + ls_pos
    mx = jnp.maximum(log_lb, c)
    logf = mx + jnp.log1p(jnp.exp(-jnp.abs(log_lb - c)))
    key = jnp.exp(log_1mlb + ls_neg)
    return logf, key


def _gla_kernel(q_ref, zf_ref, zb_ref, v_ref, g_ref, llb_ref, l1m_ref, gain_ref, o_ref, os_, *, seq_len):
    C = GLA_CHUNK
    n_chunks = seq_len // C
    dk = C_HEAD_DIM
    ri = lax.broadcasted_iota(jnp.int32, (C, C), 0)
    ci = lax.broadcasted_iota(jnp.int32, (C, C), 1)
    lower = ri >= ci
    upper = ri <= ci
    tril = jnp.where(lower, 1.0, 0.0).astype(BF16)
    triu = jnp.where(upper, 1.0, 0.0).astype(BF16)
    log_lb = llb_ref[...]
    log_1mlb = l1m_ref[...]

    def silu_q(sl):
        q = q_ref[sl, :].astype(F32)
        return q * jax.nn.sigmoid(q)

    def direction(sl, z_ref, tri, mask, ref_row, last_row, st):
        q = silu_q(sl)
        v = v_ref[sl, :]
        logf, key = _gate(z_ref[sl, :].astype(F32), log_lb, log_1mlb)
        b = _tri_cumsum(tri, logf)
        b_ref = b[ref_row:ref_row + 1, :]
        b_last = b[last_row:last_row + 1, :]
        qt = (q * jnp.exp(b - b_ref)).astype(BF16)
        kt = (key * jnp.exp(b_ref - b)).astype(BF16)
        att = jnp.where(mask, _dot_nt(qt, kt), 0.0).astype(BF16)
        o = _dot(att, v) + _dot_nt((q * jnp.exp(b)).astype(BF16), st.astype(BF16))
        vt = v.astype(F32).T.astype(BF16)
        kv_t = _dot(vt, (key * jnp.exp(b_last - b)).astype(BF16))
        return o, st * jnp.exp(b_last) + kv_t

    def fwd(n, st):
        sl = pl.ds(pl.multiple_of(n * C, C), C)
        o, st = direction(sl, zf_ref, tril, lower, C // 2 - 1, C - 1, st)
        os_[sl, :] = o
        return st

    lax.fori_loop(0, n_chunks, fwd, jnp.zeros((dk, dk), F32))

    def bwd(i, st):
        n = n_chunks - 1 - i
        sl = pl.ds(pl.multiple_of(n * C, C), C)
        o, st = direction(sl, zb_ref, triu, upper, C // 2, 0, st)
        o = o + os_[sl, :]
        y = o * lax.rsqrt(jnp.mean(o * o, axis=-1, keepdims=True) + EPS) * gain_ref[...]
        g = g_ref[sl, :].astype(F32)
        o_ref[sl, :] = (g * jax.nn.sigmoid(g) * y).astype(o_ref.dtype)
        return st

    lax.fori_loop(0, n_chunks, bwd, jnp.zeros((dk, dk), F32))


def _mixer_c(z3, log_lb, log_1mlb, gain):
    Bn, S, _ = z3.shape

    def zspec(slab):
        return pl.BlockSpec((None, S, LANE), lambda b, h: (b, 0, COL_C + slab * HEAD_COLS + h))

    vec = pl.BlockSpec((1, LANE), lambda b, h: (0, h))
    return pl.pallas_call(
        functools.partial(_gla_kernel, seq_len=S),
        out_shape=jax.ShapeDtypeStruct((Bn, S, C_WIDTH), BF16),
        grid=(Bn, C_HEADS),
        in_specs=[zspec(0), zspec(1), zspec(2), zspec(3), zspec(4), vec, vec, vec],
        out_specs=pl.BlockSpec((None, S, LANE), lambda b, h: (b, 0, h)),
        scratch_shapes=[pltpu.VMEM((S, LANE), F32)],
        compiler_params=_params("parallel", "arbitrary"),
        name="hgrn2",
    )(z3, z3, z3, z3, z3, log_lb, log_1mlb, gain)


def _merge_kernel(h_ref, o0, o1, o2, l0, l1, l2, b_ref, c_ref, gt_ref, wa_ref, wb_ref, wc_ref, wo_ref, out_ref):
    ls = (l0[...], l1[...], l2[...])
    mx = jnp.maximum(jnp.maximum(ls[0], ls[1]), ls[2])
    es = [jnp.exp(l - mx) for l in ls]
    den = es[0] + es[1] + es[2]
    a = (es[0] * o0[...] + es[1] * o1[...] + es[2] * o2[...]) / den
    D = D_MODEL
    m = (gt_ref[:, 0:D].astype(F32) * _dot(a.astype(BF16), wa_ref[...])
         + gt_ref[:, D:2 * D].astype(F32) * _dot(b_ref[...], wb_ref[...])
         + gt_ref[:, 2 * D:3 * D].astype(F32) * _dot(c_ref[...], wc_ref[...]))
    out_ref[...] = h_ref[...] + _dot(m.astype(BF16), wo_ref[...])


def _merge(h, outs, lses, b, c, gates, wa, wb, wc, wo, layer):
    T = h.shape[0]
    tm = TOKEN_TILE
    row = lambda w: pl.BlockSpec((tm, w), lambda i: (i, 0))
    wspec = lambda k: pl.BlockSpec((None, k, D_MODEL), lambda i: (layer, 0, 0))
    return pl.pallas_call(
        _merge_kernel,
        out_shape=jax.ShapeDtypeStruct((T, D_MODEL), F32),
        grid=(T // tm,),
        in_specs=[row(D_MODEL)] + [row(A_GROUP_WIDTH)] * 6 + [row(B_WIDTH), row(C_WIDTH), row(3 * D_MODEL),
                  wspec(A_GROUP_WIDTH), wspec(B_WIDTH), wspec(C_WIDTH), wspec(D_MODEL)],
        out_specs=row(D_MODEL),
        compiler_params=_params("parallel"),
        name="merge",
    )(h, *outs, *lses, b, c, gates, wa, wb, wc, wo)


def _ple_kernel(h_ref, p_ref, g_ref, wg_ref, wp_ref, fg_ref, out_ref, *, final):
    h = h_ref[...]
    gate = jax.nn.sigmoid(_dot(_rms_bf16(h, g_ref[...]), wg_ref[...]))
    h = h + gate * _dot(p_ref[...].astype(BF16), wp_ref[...])
    if final:
        ms = jnp.mean(h * h, axis=-1, keepdims=True)
        h = h * lax.rsqrt(ms + EPS) * fg_ref[...]
    out_ref[...] = h


def _ple(h, p, g, wg, wp, final_g, layer, final):
    T = h.shape[0]
    tm = TOKEN_TILE
    return pl.pallas_call(
        functools.partial(_ple_kernel, final=final),
        out_shape=jax.ShapeDtypeStruct((T, D_MODEL), F32),
        grid=(T // tm,),
        in_specs=[
            pl.BlockSpec((tm, D_MODEL), lambda i: (i, 0)),
            pl.BlockSpec((None, tm, D_PLE), lambda i: (layer, i, 0)),
            pl.BlockSpec((None, 1, D_MODEL), lambda i: (layer, 0, 0)),
            pl.BlockSpec((None, D_MODEL, D_MODEL), lambda i: (layer, 0, 0)),
            pl.BlockSpec((None, D_PLE, D_MODEL), lambda i: (layer, 0, 0)),
            pl.BlockSpec((1, D_MODEL), lambda i: (0, 0)),
        ],
        out_specs=pl.BlockSpec((tm, D_MODEL), lambda i: (i, 0)),
        compiler_params=_params("parallel"),
        name="ple",
    )(h, p, g, wg, wp, final_g)


def _rope_tables(S):
    half = B_HEAD_DIM // 2
    inv = ROPE_BASE ** (-jnp.arange(half, dtype=F32) / half)
    ang = jnp.arange(S, dtype=F32)[:, None] * inv[None]
    cos, sin = jnp.cos(ang), jnp.sin(ang)
    return jnp.concatenate([cos, cos], axis=-1), jnp.concatenate([-sin, sin], axis=-1)


def _trunk(x, p, w):
    Bn, S, D = x.shape
    T = Bn * S
    h = x.reshape(T, D)
    p = p.reshape(DEPTH, T, D_PLE)
    cos2, sin2 = _rope_tables(S)
    ret_consts = _ret_consts()
    for l in range(DEPTH):
        h = _ffn(h, w["ffn1_norm"], w["ffn1_w_gate"], w["ffn1_w_up"], w["ffn1_w_down"], l)
        z = _rms_proj(h, w["mix_norm"], w["w_in"], l, False, "proj_in")
        gates = _rms_proj(h, w["mix_norm"], w["w_merge_gate"], l, True, "proj_gate")
        z3 = z.reshape(Bn, S, N_IN)
        outs, lses = _mixer_a(z3, w["rel_bias"])
        b = _mixer_b(z3, cos2, sin2, ret_consts, w["ret_norm"][l][None]).reshape(T, B_WIDTH)
        c = _mixer_c(z3, w["log_lb"][l][None], w["log_1mlb"][l][None], w["hgrn_norm"][l][None]).reshape(T, C_WIDTH)
        h = _merge(h, outs, lses, b, c, gates, w["w_branch_a"], w["w_branch_b"], w["w_branch_c"], w["w_out"], l)
        h = _ffn(h, w["ffn2_norm"], w["ffn2_w_gate"], w["ffn2_w_up"], w["ffn2_w_down"], l)
        h = _ple(h, p, w["ple_norm"], w["w_ple_gate"], w["w_ple_proj"], w["final_norm"], l, l == DEPTH - 1)
    return h.reshape(Bn, S, D)


def kernel(x_prompt, x_sample, p_prompt, p_sample, ffn1_norm, ffn1_w_gate, ffn1_w_up, ffn1_w_down, mix_norm, w_in, rel_bias, ret_norm, hgrn_lower_bound, hgrn_norm, w_branch_a, w_branch_b, w_branch_c, w_merge_gate, w_out, ffn2_norm, ffn2_w_gate, ffn2_w_up, ffn2_w_down, ple_norm, w_ple_gate, w_ple_proj, final_norm):
    lb = jax.nn.softmax(hgrn_lower_bound.astype(F32), axis=0)
    lb = jnp.cumsum(lb, axis=0)
    lbs = lb - lb[0]
    bf = lambda a: a.astype(BF16)
    vec = lambda a: a.astype(F32)[:, None, :]
    w = dict(
        ffn1_norm=vec(ffn1_norm), ffn1_w_gate=bf(ffn1_w_gate), ffn1_w_up=bf(ffn1_w_up), ffn1_w_down=bf(ffn1_w_down),
        mix_norm=vec(mix_norm), w_in=bf(w_in), rel_bias=rel_bias, ret_norm=ret_norm.astype(F32),
        hgrn_norm=hgrn_norm.astype(F32), log_lb=jnp.log(lbs), log_1mlb=jnp.log1p(-lbs),
        w_branch_a=bf(w_branch_a), w_branch_b=bf(w_branch_b), w_branch_c=bf(w_branch_c),
        w_merge_gate=bf(w_merge_gate), w_out=bf(w_out),
        ffn2_norm=vec(ffn2_norm), ffn2_w_gate=bf(ffn2_w_gate), ffn2_w_up=bf(ffn2_w_up), ffn2_w_down=bf(ffn2_w_down),
        ple_norm=vec(ple_norm), w_ple_gate=bf(w_ple_gate), w_ple_proj=bf(w_ple_proj),
        final_norm=final_norm.astype(F32)[None, :],
    )
    return (_trunk(x_prompt, p_prompt, w), _trunk(x_sample, p_sample, w))
```

```python
import functools
import math

import jax
import jax.numpy as jnp
import numpy as np
from jax import lax
from jax.experimental import pallas as pl
from jax.experimental.pallas import tpu as pltpu

F32 = jnp.float32
BF16 = jnp.bfloat16

EPS = 1e-6
D_MODEL = 1024
D_PLE = 256
D_FF = 2816
DEPTH = 4
A_GROUPS = ((128, 1), (512, 4), (2048, 16))
A_HEADS_PER_GROUP = 4
A_HEAD_DIM = 64
A_HEADS = 12
A_GROUP_WIDTH = A_HEADS_PER_GROUP * A_HEAD_DIM
A_WIDTH = 768
A_BLOCK = 64
A_BLOCKS_PER_STEP = 4
N_BUCKETS = 32
MAX_DISTANCE = 1024
B_HEADS = 4
B_HEAD_DIM = 128
B_WIDTH = 512
RET_TILE = 256
ROPE_BASE = 10000.0
C_HEADS = 4
C_HEAD_DIM = 128
C_WIDTH = 512
GLA_CHUNK = 64
GLA_TILE = 256
N_IN = 3 * A_WIDTH + 4 * B_WIDTH + 5 * C_WIDTH
assert A_GROUPS[0][1] == 1 and all(d > 1 for _, d in A_GROUPS[1:])
LANE = 128
COL_B = 0
COL_C = COL_B + 4 * B_WIDTH // LANE
HEAD_COLS = B_WIDTH // LANE

VMEM_LIMIT_BYTES = 56 * 1024 * 1024
TOKEN_TILE = 512
FFN_F_TILE = 1408
PROJ_N_TILE = 768


def _params(*sem):
    return pltpu.CompilerParams(dimension_semantics=sem, vmem_limit_bytes=VMEM_LIMIT_BYTES)


def _rms_bf16(x, g):
    ms = jnp.mean(x * x, axis=-1, keepdims=True)
    return (x * lax.rsqrt(ms + EPS) * g).astype(BF16)


def _dot(a, b):
    return jnp.dot(a, b, preferred_element_type=F32)


def _dot_nt(a, b):
    return lax.dot_general(a, b, (((1,), (1,)), ((), ())), preferred_element_type=F32)


def _dot_tn(a, b):
    return lax.dot_general(a, b, (((0,), (0,)), ((), ())), preferred_element_type=F32)


def _ffn_kernel(x_ref, g_ref, wg_ref, wu_ref, wd_ref, o_ref, u_scr, acc_scr):
    f = pl.program_id(1)

    @pl.when(f == 0)
    def _():
        u_scr[...] = _rms_bf16(x_ref[...], g_ref[...])
        acc_scr[...] = jnp.zeros_like(acc_scr)

    u = u_scr[...]
    a = _dot(u, wg_ref[...])
    b = _dot(u, wu_ref[...])
    hid = (a * jax.nn.sigmoid(a) * b).astype(BF16)
    acc_scr[...] += _dot(hid, wd_ref[...])

    @pl.when(f == pl.num_programs(1) - 1)
    def _():
        o_ref[...] = x_ref[...] + 0.5 * acc_scr[...]


def _ffn(h, g, wg, wu, wd, layer):
    T = h.shape[0]
    tm, tf = TOKEN_TILE, FFN_F_TILE
    return pl.pallas_call(
        _ffn_kernel,
        out_shape=jax.ShapeDtypeStruct((T, D_MODEL), F32),
        grid=(T // tm, D_FF // tf),
        in_specs=[
            pl.BlockSpec((tm, D_MODEL), lambda i, f: (i, 0)),
            pl.BlockSpec((None, 1, D_MODEL), lambda i, f: (layer, 0, 0)),
            pl.BlockSpec((None, D_MODEL, tf), lambda i, f: (layer, 0, f)),
            pl.BlockSpec((None, D_MODEL, tf), lambda i, f: (layer, 0, f)),
            pl.BlockSpec((None, tf, D_MODEL), lambda i, f: (layer, f, 0)),
        ],
        out_specs=pl.BlockSpec((tm, D_MODEL), lambda i, f: (i, 0)),
        scratch_shapes=[pltpu.VMEM((tm, D_MODEL), BF16), pltpu.VMEM((tm, D_MODEL), F32)],
        compiler_params=_params("parallel", "arbitrary"),
        name="ffn",
    )(h, g, wg, wu, wd)


def _proj_kernel(x_ref, g_ref, win_ref, wgt_ref, za0_ref, za1_ref, za2_ref, zbc_ref, gt_ref, stage):
    tn = PROJ_N_TILE
    tm = TOKEN_TILE
    W = A_GROUP_WIDTH
    za_refs = (za0_ref, za1_ref, za2_ref)
    u = _rms_bf16(x_ref[...], g_ref[...])
    for j in range(3):
        res = _dot(u, win_ref[:, j * A_WIDTH:(j + 1) * A_WIDTH])
        for grp, (_, dil) in enumerate(A_GROUPS):
            dst = za_refs[grp]
            if dil == 1:
                dst[0, :, j * W:(j + 1) * W] = res[:, grp * W:(grp + 1) * W].astype(BF16)
                continue
            halves = W // LANE
            for c in range(halves):
                stage[j % 2, grp - 1, c] = res[:, grp * W + c * LANE:grp * W + (c + 1) * LANE]
            for r in range(dil):
                for c in range(halves):
                    sub = stage[j % 2, grp - 1, c, pl.ds(r, tm // dil, stride=dil), :]
                    dst[r, :, j * W + c * LANE:j * W + (c + 1) * LANE] = sub.astype(BF16)
    for j in range((N_IN - 3 * A_WIDTH) // tn):
        cols = slice(3 * A_WIDTH + j * tn, 3 * A_WIDTH + (j + 1) * tn)
        zbc_ref[:, j * tn:(j + 1) * tn] = _dot(u, win_ref[:, cols]).astype(BF16)
    for j in range(3 * D_MODEL // tn):
        cols = slice(j * tn, (j + 1) * tn)
        gt_ref[:, cols] = jax.nn.sigmoid(_dot(u, wgt_ref[:, cols])).astype(BF16)


def _rms_proj(h, g, w_in, w_gate, layer, Bn, S):
    T = h.shape[0]
    tm = TOKEN_TILE
    tiles = S // tm
    n_bc = N_IN - 3 * A_WIDTH
    resident = lambda n: pl.BlockSpec((None, D_MODEL, n), lambda i: (layer, 0, 0), pipeline_mode=pl.Buffered(1))
    za_shapes, za_specs = [], []
    for _, dil in A_GROUPS:
        assert tm % (dil * 16) == 0 and S % tm == 0
        za_shapes.append(jax.ShapeDtypeStruct((Bn, dil, S // dil, A_WIDTH), BF16))
        za_specs.append(pl.BlockSpec((None, dil, tm // dil, A_WIDTH), lambda i: (i // tiles, 0, i % tiles, 0)))
    return pl.pallas_call(
        _proj_kernel,
        out_shape=(*za_shapes, jax.ShapeDtypeStruct((T, n_bc), BF16), jax.ShapeDtypeStruct((T, 3 * D_MODEL), BF16)),
        grid=(T // tm,),
        in_specs=[
            pl.BlockSpec((tm, D_MODEL), lambda i: (i, 0)),
            pl.BlockSpec((None, 1, D_MODEL), lambda i: (layer, 0, 0)),
            resident(N_IN),
            resident(3 * D_MODEL),
        ],
        out_specs=(*za_specs, pl.BlockSpec((tm, n_bc), lambda i: (i, 0)), pl.BlockSpec((tm, 3 * D_MODEL), lambda i: (i, 0))),
        scratch_shapes=[pltpu.VMEM((2, len(A_GROUPS) - 1, A_GROUP_WIDTH // LANE, tm, LANE), F32)],
        compiler_params=_params("parallel"),
        name="proj_in",
    )(h, g, w_in, w_gate)


def _t5_bucket_np(rel):
    half = N_BUCKETS // 2
    max_exact = half // 2
    ret = np.where(rel > 0, half, 0)
    n = np.abs(rel)
    nf = np.maximum(n, 1).astype(np.float32)
    large = max_exact + (
        np.log(nf / np.float32(max_exact)) / np.float32(math.log(MAX_DISTANCE / max_exact)) * np.float32(half - max_exact)
    ).astype(np.int32)
    large = np.minimum(large, half - 1)
    return ret + np.where(n < max_exact, n, large)


def _band_attn_kernel(q_ref, kin_ref, vin_ref, bias_ref, o_ref, lse_ref, k_ref, v_ref, *, seq_len):
    C = A_BLOCK
    H = A_HEADS_PER_GROUP
    hd = A_HEAD_DIM
    W = A_GROUP_WIDTH
    n_blocks = seq_len // C
    scale = hd ** -0.5
    assert math.log2(scale) == round(math.log2(scale))

    for src, dst in ((kin_ref, k_ref), (vin_ref, v_ref)):
        dst[0:C, :] = jnp.zeros((C, W), BF16)
        dst[C + seq_len:2 * C + seq_len, :] = jnp.zeros((C, W), BF16)
        dst[C:C + seq_len, :] = src[...]

    lane_head_q = lax.broadcasted_iota(jnp.int32, (C, W), 1) // hd

    def scores(n):
        base = pl.multiple_of(n * C, C)
        qb = q_ref[pl.ds(base, C), :].astype(F32) * scale
        kb = k_ref[pl.ds(base, 3 * C), :]
        qexp = jnp.concatenate([jnp.where(lane_head_q == j, qb, 0.0) for j in range(H)], axis=0).astype(BF16)
        return _dot_nt(qexp, kb)

    def softmax(n, s):
        edge = jnp.where(n == 0, 1, 0) + jnp.where(n == n_blocks - 1, 2, 0)
        s = s + bias_ref[edge]
        m = jnp.max(s, axis=-1, keepdims=True)
        e = jnp.exp(s - m)
        den = jnp.sum(e, axis=-1, keepdims=True)
        return (e / den).astype(BF16), m + jnp.log(den)

    def output(n, p, lse):
        base = pl.multiple_of(n * C, C)
        oall = _dot(p, v_ref[pl.ds(base, 3 * C), :])
        o = jnp.zeros((C, W), F32)
        lse_b = jnp.zeros((C, W), F32)
        for j in range(H):
            sel = lane_head_q == j
            o = jnp.where(sel, oall[j * C:(j + 1) * C, :], o)
            lse_b = jnp.where(sel, lse[j * C:(j + 1) * C, :], lse_b)
        o_ref[pl.ds(base, C), :] = o
        lse_ref[pl.ds(base, C), :] = lse_b

    nb = math.gcd(n_blocks, A_BLOCKS_PER_STEP)

    def body(i, carry):
        ns = [i * nb + t for t in range(nb)]
        ss = [scores(n) for n in ns]
        pls = [softmax(n, s) for n, s in zip(ns, ss)]
        for n, (p, lse) in zip(ns, pls):
            output(n, p, lse)
        return carry

    lax.fori_loop(0, n_blocks // nb, body, 0)


def _band_attn(za, bias):
    Bn, dil, L, _ = za.shape
    C = A_BLOCK
    W = A_GROUP_WIDTH
    kern = functools.partial(_band_attn_kernel, seq_len=L)
    col = lambda c: pl.BlockSpec((None, None, L, W), lambda i: (i // dil, i % dil, 0, c))
    out = jax.ShapeDtypeStruct((Bn, dil, L, W), F32)
    return pl.pallas_call(
        kern,
        out_shape=(out, out),
        grid=(Bn * dil,),
        in_specs=[col(0), col(1), col(2), pl.BlockSpec((4, A_HEADS_PER_GROUP * C, 3 * C), lambda i: (0, 0, 0))],
        out_specs=(col(0), col(0)),
        scratch_shapes=[pltpu.VMEM((L + 2 * C, W), BF16), pltpu.VMEM((L + 2 * C, W), BF16)],
        compiler_params=_params("parallel"),
        name="band_attn",
    )(za, za, za, bias)


def _mixer_a(zas, rel_bias):
    C = A_BLOCK
    outs, lses = [], []
    qi = np.arange(C)[:, None]
    kj = np.arange(3 * C)[None, :]
    off = kj - C - qi
    for g, (window, dil) in enumerate(A_GROUPS):
        n_side = (window // 2) // dil
        L = zas[g].shape[2]
        assert L % C == 0
        bucket = _t5_bucket_np((off * dil).astype(np.int32))
        tbl = rel_bias[:, g * A_HEADS_PER_GROUP:(g + 1) * A_HEADS_PER_GROUP].astype(F32)
        bias = jnp.transpose(tbl[bucket], (2, 0, 1))
        band = np.abs(off) <= n_side
        left, right = kj >= C, kj < 2 * C
        masks = np.stack([band, band & left, band & right, band & left & right])
        bias = jnp.where(masks[:, None], bias[None], -1e30).reshape(4, A_HEADS_PER_GROUP * C, 3 * C)
        o, lse = _band_attn(zas[g], bias)
        outs.append(o)
        lses.append(lse)
    return outs, lses


def _ret_consts():
    C = RET_TILE
    hh = jnp.arange(B_HEADS, dtype=F32)
    lgf = jnp.log1p(-jnp.exp2(-5.0 - hh))[:, None, None]
    lgb = jnp.log1p(-jnp.exp2(-5.5 - hh))[:, None, None]
    i = jnp.arange(C, dtype=F32)[None, :, None]
    j = jnp.arange(C, dtype=F32)[None, None, :]
    rel = i - j
    dfb = (jnp.where(rel >= 0, jnp.exp(jnp.maximum(rel, 0.0) * lgf), 0.0)
           + jnp.where(rel <= 0, jnp.exp(jnp.maximum(-rel, 0.0) * lgb), 0.0))
    ones = jnp.ones((1, 1, LANE), F32)
    rows = jnp.stack([jnp.exp((i + 1.0) * lgf) * ones,
                      jnp.exp((C - i) * lgb) * ones,
                      jnp.exp((C - 1.0 - i) * lgf) * ones,
                      jnp.exp(i * lgb) * ones], axis=1)
    chunk = jnp.concatenate([jnp.exp(C * lgf) * ones, jnp.exp(C * lgb) * ones], axis=1)
    return dfb, rows, chunk


def _ret_kernel(q_ref, k_ref, v_ref, g_ref, cos_ref, sin_ref, d_ref, r_ref, c_ref, gain_ref, o_ref,
                os_, qh_s, kvt_s, st_s, *, seq_len):
    C = RET_TILE
    n_chunks = seq_len // C
    dv = B_HEAD_DIM
    scale = B_HEAD_DIM ** -0.5

    def rot(x, sl):
        return x * cos_ref[sl, :] + pltpu.roll(x, B_HEAD_DIM // 2, 1) * sin_ref[sl, :]

    nb = math.gcd(n_chunks, 2)

    def local(i, carry):
        ns = [i * nb + t for t in range(nb)]
        sls = [pl.ds(pl.multiple_of(n * C, C), C) for n in ns]
        qs = [rot(q_ref[sl, :].astype(F32), sl) for sl in sls]
        ks = [rot(k_ref[sl, :].astype(F32), sl) * scale for sl in sls]
        scs = [_dot_nt(q.astype(BF16), k.astype(BF16)) for q, k in zip(qs, ks)]
        khats = [jnp.concatenate([(k * r_ref[2]).astype(BF16), (k * r_ref[3]).astype(BF16)], axis=1) for k in ks]
        for n, sl, khat in zip(ns, sls, khats):
            kvt_s[n] = _dot_tn(v_ref[sl, :], khat)
        for sl, q in zip(sls, qs):
            qh_s[sl, 0:dv] = (q * r_ref[0]).astype(BF16)
            qh_s[sl, dv:2 * dv] = (q * r_ref[1]).astype(BF16)
        for sl, s in zip(sls, scs):
            os_[sl, :] = _dot((s * d_ref[...]).astype(BF16), v_ref[sl, :])
        return carry

    lax.fori_loop(0, n_chunks // nb, local, 0)

    gcf = c_ref[0:1, :]
    gcb = c_ref[1:2, :]

    def scan(i, carry):
        sf, sb = carry
        nb = n_chunks - 1 - i
        st_s[i, :, 0:dv] = sf.astype(BF16)
        sf = sf * gcf + kvt_s[i, :, 0:dv]
        st_s[nb, :, dv:2 * dv] = sb.astype(BF16)
        sb = sb * gcb + kvt_s[nb, :, dv:2 * dv]
        return sf, sb

    zero = jnp.zeros((dv, dv), F32)
    lax.fori_loop(0, n_chunks, scan, (zero, zero))

    def finish(i, carry):
        ns = [i * nb + t for t in range(nb)]
        sls = [pl.ds(pl.multiple_of(n * C, C), C) for n in ns]
        outs = [os_[sl, :] + _dot_nt(qh_s[sl, :], st_s[n]) for n, sl in zip(ns, sls)]
        for sl, o in zip(sls, outs):
            mu = jnp.mean(o, axis=-1, keepdims=True)
            d = o - mu
            var = jnp.mean(d * d, axis=-1, keepdims=True)
            y = d * lax.rsqrt(var + EPS) * gain_ref[...]
            g = g_ref[sl, :].astype(F32)
            o_ref[sl, :] = (g * jax.nn.sigmoid(g) * y).astype(o_ref.dtype)
        return carry

    lax.fori_loop(0, n_chunks // nb, finish, 0)


def _mixer_b(z3, cos2, sin2, consts, gain):
    Bn, S, _ = z3.shape
    C = RET_TILE
    dfb, rows, chunk = consts

    def zspec(slab):
        return pl.BlockSpec((None, S, LANE), lambda b, h: (b, 0, COL_B + slab * HEAD_COLS + h))

    return pl.pallas_call(
        functools.partial(_ret_kernel, seq_len=S),
        out_shape=jax.ShapeDtypeStruct((Bn, S, B_WIDTH), BF16),
        grid=(Bn, B_HEADS),
        in_specs=[
            zspec(0), zspec(1), zspec(2), zspec(3),
            pl.BlockSpec((S, LANE), lambda b, h: (0, 0)),
            pl.BlockSpec((S, LANE), lambda b, h: (0, 0)),
            pl.BlockSpec((None, C, C), lambda b, h: (h, 0, 0)),
            pl.BlockSpec((None, 4, C, LANE), lambda b, h: (h, 0, 0, 0)),
            pl.BlockSpec((None, 2, LANE), lambda b, h: (h, 0, 0)),
            pl.BlockSpec((1, LANE), lambda b, h: (0, h)),
        ],
        out_specs=pl.BlockSpec((None, S, LANE), lambda b, h: (b, 0, h)),
        scratch_shapes=[
            pltpu.VMEM((S, LANE), F32),
            pltpu.VMEM((S, 2 * LANE), BF16),
            pltpu.VMEM((S // C, LANE, 2 * LANE), F32),
            pltpu.VMEM((S // C, LANE, 2 * LANE), BF16),
        ],
        compiler_params=_params("parallel", "arbitrary"),
        name="retention",
    )(z3, z3, z3, z3, cos2, sin2, dfb, rows, chunk, gain)


def _split3(x):
    hi = x.astype(BF16)
    r1 = x - hi.astype(F32)
    mid = r1.astype(BF16)
    lo = (r1 - mid.astype(F32)).astype(BF16)
    return hi, mid, lo


def _tri_cumsum(tri, x):
    hi, mid, lo = _split3(x)
    return _dot(tri, hi) + _dot(tri, mid) + _dot(tri, lo)


def _gate(z, log_lb, log_1mlb):
    t = jnp.log(1.0 + jnp.exp(-jnp.abs(z)))
    ls_pos = jnp.minimum(z, 0.0) - t
    ls_neg = jnp.minimum(-z, 0.0) - t
    c = log_1mlb + ls_pos
    mx = jnp.maximum(log_lb, c)
    logf = mx + jnp.log(1.0 + jnp.exp(-jnp.abs(log_lb - c)))
    key = jnp.exp(log_1mlb + ls_neg)
    return logf, key


def _gla_masks():
    i = np.arange(GLA_TILE)
    same = (i[:, None] // GLA_CHUNK) == (i[None, :] // GLA_CHUNK)
    m = np.stack([same & (i[:, None] >= i[None, :]), same & (i[:, None] <= i[None, :])]).astype(np.float32)
    return jnp.asarray(m, F32), jnp.asarray(m, BF16)


def _gla_kernel(q_ref, zf_ref, zb_ref, v_ref, g_ref, llb_ref, l1m_ref, gain_ref, mask_ref, tri_ref, o_ref,
                os_, qh_s, kvt_s, st_s, gf_s, gb_s, *, seq_len):
    C = GLA_CHUNK
    R = GLA_TILE
    NC = R // C
    n_tiles = seq_len // R
    n_chunks = seq_len // C
    dk = C_HEAD_DIM
    log_lb = llb_ref[...]
    log_1mlb = l1m_ref[...]

    def scaled(q3, key, b, ref_row, last_row):
        b3 = b.reshape(NC, C, dk)
        key3 = key.reshape(NC, C, dk)
        b_ref = b3[:, ref_row:ref_row + 1, :]
        b_last = b3[:, last_row:last_row + 1, :]
        qt = (q3 * jnp.exp(b3 - b_ref)).reshape(R, dk).astype(BF16)
        kt = (key3 * jnp.exp(b_ref - b3)).reshape(R, dk).astype(BF16)
        qh = (q3 * jnp.exp(b3)).reshape(R, dk).astype(BF16)
        kh = (key3 * jnp.exp(b_last - b3)).reshape(R, dk).astype(BF16)
        return qt, kt, qh, kh, jnp.exp(b_last).reshape(NC, dk)

    nt = math.gcd(n_tiles, 2)

    def local(i, carry):
        ts = [i * nt + u for u in range(nt)]
        sls = [pl.ds(pl.multiple_of(t * R, R), R) for t in ts]
        gates_f = [_gate(zf_ref[sl, :].astype(F32), log_lb, log_1mlb) for sl in sls]
        gates_b = [_gate(zb_ref[sl, :].astype(F32), log_lb, log_1mlb) for sl in sls]
        bfs = [_tri_cumsum(tri_ref[0], logf) for logf, _ in gates_f]
        bbs = [_tri_cumsum(tri_ref[1], logf) for logf, _ in gates_b]
        q3s = []
        for sl in sls:
            q = q_ref[sl, :].astype(F32)
            q3s.append((q * jax.nn.sigmoid(q)).reshape(NC, C, dk))
        fwd = [scaled(q3, key, b, C // 2 - 1, C - 1) for q3, (_, key), b in zip(q3s, gates_f, bfs)]
        bwd = [scaled(q3, key, b, C // 2, 0) for q3, (_, key), b in zip(q3s, gates_b, bbs)]
        sc_f = [_dot_nt(f[0], f[1]) for f in fwd]
        sc_b = [_dot_nt(b[0], b[1]) for b in bwd]
        for t, sl, f, b in zip(ts, sls, fwd, bwd):
            v = v_ref[sl, :]
            khat = jnp.concatenate([f[3], b[3]], axis=1)
            cbase = t * NC
            for c in range(NC):
                rows = slice(c * C, (c + 1) * C)
                kvt_s[cbase + c] = _dot_tn(v[rows, :], khat[rows, :])
            qh_s[sl, 0:dk] = f[2]
            qh_s[sl, dk:2 * dk] = b[2]
            gf_s[pl.ds(cbase, NC), :] = f[4]
            gb_s[pl.ds(cbase, NC), :] = b[4]
        for sl, sf, sb in zip(sls, sc_f, sc_b):
            att = jnp.where(mask_ref[0] > 0.5, sf, 0.0) + jnp.where(mask_ref[1] > 0.5, sb, 0.0)
            os_[sl, :] = _dot(att.astype(BF16), v_ref[sl, :])
        return carry

    lax.fori_loop(0, n_tiles // nt, local, 0)

    def scan(i, carry):
        sf, sb = carry
        nb = n_chunks - 1 - i
        st_s[i, :, 0:dk] = sf.astype(BF16)
        sf = sf * gf_s[pl.ds(i, 1), :] + kvt_s[i, :, 0:dk]
        st_s[nb, :, dk:2 * dk] = sb.astype(BF16)
        sb = sb * gb_s[pl.ds(nb, 1), :] + kvt_s[nb, :, dk:2 * dk]
        return sf, sb

    zero = jnp.zeros((dk, dk), F32)
    lax.fori_loop(0, n_chunks, scan, (zero, zero), unroll=2)

    nb = math.gcd(n_tiles, 2)

    def finish(i, carry):
        outs = []
        for t in [i * nb + u for u in range(nb)]:
            base = pl.multiple_of(t * R, R)
            cbase = t * NC
            parts = []
            for c in range(NC):
                rows = pl.ds(base + c * C, C)
                parts.append(os_[rows, :] + _dot_nt(qh_s[rows, :], st_s[cbase + c]))
            outs.append((pl.ds(base, R), jnp.concatenate(parts, axis=0)))
        for sl, o in outs:
            y = o * lax.rsqrt(jnp.mean(o * o, axis=-1, keepdims=True) + EPS) * gain_ref[...]
            g = g_ref[sl, :].astype(F32)
            o_ref[sl, :] = (g * jax.nn.sigmoid(g) * y).astype(o_ref.dtype)
        return carry

    lax.fori_loop(0, n_tiles // nb, finish, 0)


def _mixer_c(z3, log_lb, log_1mlb, gain):
    Bn, S, _ = z3.shape
    n_chunks = S // GLA_CHUNK

    def zspec(slab):
        return pl.BlockSpec((None, S, LANE), lambda b, h: (b, 0, COL_C + slab * HEAD_COLS + h))

    vec = pl.BlockSpec((1, LANE), lambda b, h: (0, h))
    tile_const = pl.BlockSpec((2, GLA_TILE, GLA_TILE), lambda b, h: (0, 0, 0))
    mask_f32, mask_bf16 = _gla_masks()
    return pl.pallas_call(
        functools.partial(_gla_kernel, seq_len=S),
        out_shape=jax.ShapeDtypeStruct((Bn, S, C_WIDTH), BF16),
        grid=(Bn, C_HEADS),
        in_specs=[zspec(0), zspec(1), zspec(2), zspec(3), zspec(4), vec, vec, vec, tile_const, tile_const],
        out_specs=pl.BlockSpec((None, S, LANE), lambda b, h: (b, 0, h)),
        scratch_shapes=[
            pltpu.VMEM((S, LANE), F32),
            pltpu.VMEM((S, 2 * LANE), BF16),
            pltpu.VMEM((n_chunks, LANE, 2 * LANE), F32),
            pltpu.VMEM((n_chunks, LANE, 2 * LANE), BF16),
            pltpu.VMEM((n_chunks, LANE), F32),
            pltpu.VMEM((n_chunks, LANE), F32),
        ],
        compiler_params=_params("parallel", "arbitrary"),
        name="hgrn2",
    )(z3, z3, z3, z3, z3, log_lb, log_1mlb, gain, mask_f32, mask_bf16)


def _merge_kernel(h_ref, o0, o1, o2, l0, l1, l2, b_ref, c_ref, gt_ref, wa_ref, wb_ref, wc_ref, wo_ref, out_ref, stage):
    tm = TOKEN_TILE
    halves = A_GROUP_WIDTH // LANE

    def token_order(ref, slot, dil, c):
        cols = slice(c * LANE, (c + 1) * LANE)
        if dil == 1:
            return ref[0, :, cols]
        for r in range(dil):
            stage[slot, c, pl.ds(r, tm // dil, stride=dil), :] = ref[r, :, cols]
        return stage[slot, c]

    parts = []
    for c in range(halves):
        os_, ls = [], []
        for grp, (o_ref, l_ref) in enumerate(((o0, l0), (o1, l1), (o2, l2))):
            dil = A_GROUPS[grp][1]
            os_.append(token_order(o_ref, 2 * grp, dil, c))
            ls.append(token_order(l_ref, 2 * grp + 1, dil, c))
        mx = jnp.maximum(jnp.maximum(ls[0], ls[1]), ls[2])
        es = [jnp.exp(l - mx) for l in ls]
        den = es[0] + es[1] + es[2]
        parts.append((es[0] * os_[0] + es[1] * os_[1] + es[2] * os_[2]) / den)
    a = jnp.concatenate(parts, axis=1)
    D = D_MODEL
    m = (gt_ref[:, 0:D].astype(F32) * _dot(a.astype(BF16), wa_ref[...])
         + gt_ref[:, D:2 * D].astype(F32) * _dot(b_ref[...], wb_ref[...])
         + gt_ref[:, 2 * D:3 * D].astype(F32) * _dot(c_ref[...], wc_ref[...]))
    out_ref[...] = h_ref[...] + _dot(m.astype(BF16), wo_ref[...])


def _merge(h, outs, lses, b, c, gates, wa, wb, wc, wo, layer, S):
    T = h.shape[0]
    tm = TOKEN_TILE
    tiles = S // tm
    row = lambda w: pl.BlockSpec((tm, w), lambda i: (i, 0))
    wspec = lambda k: pl.BlockSpec((None, k, D_MODEL), lambda i: (layer, 0, 0))
    sub = [pl.BlockSpec((None, dil, tm // dil, A_GROUP_WIDTH), lambda i: (i // tiles, 0, i % tiles, 0))
           for _, dil in A_GROUPS]
    return pl.pallas_call(
        _merge_kernel,
        out_shape=jax.ShapeDtypeStruct((T, D_MODEL), F32),
        grid=(T // tm,),
        in_specs=[row(D_MODEL)] + sub + sub + [row(B_WIDTH), row(C_WIDTH), row(3 * D_MODEL),
                  wspec(A_GROUP_WIDTH), wspec(B_WIDTH), wspec(C_WIDTH), wspec(D_MODEL)],
        out_specs=row(D_MODEL),
        scratch_shapes=[pltpu.VMEM((2 * len(A_GROUPS), A_GROUP_WIDTH // LANE, tm, LANE), F32)],
        compiler_params=_params("parallel"),
        name="merge",
    )(h, *outs, *lses, b, c, gates, wa, wb, wc, wo)


def _ple_kernel(h_ref, p_ref, g_ref, wg_ref, wp_ref, fg_ref, out_ref, *, final):
    h = h_ref[...]
    gate = jax.nn.sigmoid(_dot(_rms_bf16(h, g_ref[...]), wg_ref[...]))
    h = h + gate * _dot(p_ref[...].astype(BF16), wp_ref[...])
    if final:
        ms = jnp.mean(h * h, axis=-1, keepdims=True)
        h = h * lax.rsqrt(ms + EPS) * fg_ref[...]
    out_ref[...] = h


def _ple(h, p, g, wg, wp, final_g, layer, final):
    T = h.shape[0]
    tm = TOKEN_TILE
    return pl.pallas_call(
        functools.partial(_ple_kernel, final=final),
        out_shape=jax.ShapeDtypeStruct((T, D_MODEL), F32),
        grid=(T // tm,),
        in_specs=[
            pl.BlockSpec((tm, D_MODEL), lambda i: (i, 0)),
            pl.BlockSpec((None, tm, D_PLE), lambda i: (layer, i, 0)),
            pl.BlockSpec((None, 1, D_MODEL), lambda i: (layer, 0, 0)),
            pl.BlockSpec((None, D_MODEL, D_MODEL), lambda i: (layer, 0, 0)),
            pl.BlockSpec((None, D_PLE, D_MODEL), lambda i: (layer, 0, 0)),
            pl.BlockSpec((1, D_MODEL), lambda i: (0, 0)),
        ],
        out_specs=pl.BlockSpec((tm, D_MODEL), lambda i: (i, 0)),
        compiler_params=_params("parallel"),
        name="ple",
    )(h, p, g, wg, wp, final_g)


def _rope_tables(S):
    half = B_HEAD_DIM // 2
    inv = ROPE_BASE ** (-jnp.arange(half, dtype=F32) / half)
    ang = jnp.arange(S, dtype=F32)[:, None] * inv[None]
    cos, sin = jnp.cos(ang), jnp.sin(ang)
    return jnp.concatenate([cos, cos], axis=-1), jnp.concatenate([-sin, sin], axis=-1)


def _trunk(x, p, w):
    Bn, S, D = x.shape
    T = Bn * S
    h = x.reshape(T, D)
    p = p.reshape(DEPTH, T, D_PLE)
    cos2, sin2 = _rope_tables(S)
    ret_consts = _ret_consts()
    for l in range(DEPTH):
        h = _ffn(h, w["ffn1_norm"], w["ffn1_w_gate"], w["ffn1_w_up"], w["ffn1_w_down"], l)
        za0, za1, za2, zbc, gates = _rms_proj(h, w["mix_norm"], w["w_in"], w["w_merge_gate"], l, Bn, S)
        z3 = zbc.reshape(Bn, S, N_IN - 3 * A_WIDTH)
        outs, lses = _mixer_a((za0, za1, za2), w["rel_bias"])
        b = _mixer_b(z3, cos2, sin2, ret_consts, w["ret_norm"][l][None]).reshape(T, B_WIDTH)
        c = _mixer_c(z3, w["log_lb"][l][None], w["log_1mlb"][l][None], w["hgrn_norm"][l][None]).reshape(T, C_WIDTH)
        h = _merge(h, outs, lses, b, c, gates, w["w_branch_a"], w["w_branch_b"], w["w_branch_c"], w["w_out"], l, S)
        h = _ffn(h, w["ffn2_norm"], w["ffn2_w_gate"], w["ffn2_w_up"], w["ffn2_w_down"], l)
        h = _ple(h, p, w["ple_norm"], w["w_ple_gate"], w["w_ple_proj"], w["final_norm"], l, l == DEPTH - 1)
    return h.reshape(Bn, S, D)


def kernel(x_prompt, x_sample, p_prompt, p_sample, ffn1_norm, ffn1_w_gate, ffn1_w_up, ffn1_w_down, mix_norm, w_in, rel_bias, ret_norm, hgrn_lower_bound, hgrn_norm, w_branch_a, w_branch_b, w_branch_c, w_merge_gate, w_out, ffn2_norm, ffn2_w_gate, ffn2_w_up, ffn2_w_down, ple_norm, w_ple_gate, w_ple_proj, final_norm):
    lb = jax.nn.softmax(hgrn_lower_bound.astype(F32), axis=0)
    lb = jnp.cumsum(lb, axis=0)
    lbs = lb - lb[0]
    bf = lambda a: a.astype(BF16)
    vec = lambda a: a.astype(F32)[:, None, :]
    w = dict(
        ffn1_norm=vec(ffn1_norm), ffn1_w_gate=bf(ffn1_w_gate), ffn1_w_up=bf(ffn1_w_up), ffn1_w_down=bf(ffn1_w_down),
        mix_norm=vec(mix_norm), w_in=bf(w_in), rel_bias=rel_bias, ret_norm=ret_norm.astype(F32),
        hgrn_norm=hgrn_norm.astype(F32), log_lb=jnp.log(lbs), log_1mlb=jnp.log1p(-lbs),
        w_branch_a=bf(w_branch_a), w_branch_b=bf(w_branch_b), w_branch_c=bf(w_branch_c),
        w_merge_gate=bf(w_merge_gate), w_out=bf(w_out),
        ffn2_norm=vec(ffn2_norm), ffn2_w_gate=bf(ffn2_w_gate), ffn2_w_up=bf(ffn2_w_up), ffn2_w_down=bf(ffn2_w_down),
        ple_norm=vec(ple_norm), w_ple_gate=bf(w_ple_gate), w_ple_proj=bf(w_ple_proj),
        final_norm=final_norm.astype(F32)[None, :],
    )
    return (_trunk(x_prompt, p_prompt, w), _trunk(x_sample, p_sample, w))
```

```python
import functools
import math

import jax
import jax.numpy as jnp
import numpy as np
from jax import lax
from jax.experimental import pallas as pl
from jax.experimental.pallas import tpu as pltpu

F32 = jnp.float32
BF16 = jnp.bfloat16

EPS = 1e-6
D_MODEL = 1024
D_PLE = 256
D_FF = 2816
DEPTH = 4
A_GROUPS = ((128, 1), (512, 4), (2048, 16))
A_HEADS_PER_GROUP = 4
A_HEAD_DIM = 64
A_HEADS = 12
A_GROUP_WIDTH = A_HEADS_PER_GROUP * A_HEAD_DIM
A_WIDTH = 768
A_BLOCK = 64
A_BLOCKS_PER_STEP = 8
N_BUCKETS = 32
MAX_DISTANCE = 1024
B_HEADS = 4
B_HEAD_DIM = 128
B_WIDTH = 512
RET_TILE = 256
ROPE_BASE = 10000.0
C_HEADS = 4
C_HEAD_DIM = 128
C_WIDTH = 512
GLA_CHUNK = 64
GLA_TILE = 256
N_IN = 3 * A_WIDTH + 4 * B_WIDTH + 5 * C_WIDTH
assert A_GROUPS[0][1] == 1 and all(d > 1 for _, d in A_GROUPS[1:])
LANE = 128
COL_B = 0
COL_C = COL_B + 4 * B_WIDTH // LANE
HEAD_COLS = B_WIDTH // LANE

VMEM_LIMIT_BYTES = 56 * 1024 * 1024
TOKEN_TILE = 512
FFN_F_TILE = 256
PROJ_N_TILE = 256


def _params(*sem):
    return pltpu.CompilerParams(dimension_semantics=sem, vmem_limit_bytes=VMEM_LIMIT_BYTES)


def _rms_bf16(x, g):
    ms = jnp.mean(x * x, axis=-1, keepdims=True)
    return (x * lax.rsqrt(ms + EPS) * g).astype(BF16)


def _dot(a, b):
    return jnp.dot(a, b, preferred_element_type=F32)


def _dot_nt(a, b):
    return lax.dot_general(a, b, (((1,), (1,)), ((), ())), preferred_element_type=F32)


def _dot_tn(a, b):
    return lax.dot_general(a, b, (((0,), (0,)), ((), ())), preferred_element_type=F32)


def _swiglu_residual(x, g_ref, wg_ref, wu_ref, wd_ref):
    u = _rms_bf16(x, g_ref[...])
    acc = None
    for f in range(D_FF // FFN_F_TILE):
        cols = slice(f * FFN_F_TILE, (f + 1) * FFN_F_TILE)
        a = _dot(u, wg_ref[:, cols])
        b = _dot(u, wu_ref[:, cols])
        hid = (a * jax.nn.sigmoid(a) * b).astype(BF16)
        d = _dot(hid, wd_ref[cols, :])
        acc = d if acc is None else acc + d
    return x + 0.5 * acc


def _ffn_kernel(x_ref, g_ref, wg_ref, wu_ref, wd_ref, o_ref):
    o_ref[...] = _swiglu_residual(x_ref[...], g_ref, wg_ref, wu_ref, wd_ref)


def _ffn_ple_kernel(x_ref, g_ref, wg_ref, wu_ref, wd_ref, p_ref, pg_ref, wpg_ref, wpp_ref, fg_ref, o_ref, *, final):
    h = _swiglu_residual(x_ref[...], g_ref, wg_ref, wu_ref, wd_ref)
    gate = jax.nn.sigmoid(_dot(_rms_bf16(h, pg_ref[...]), wpg_ref[...]))
    h = h + gate * _dot(p_ref[...].astype(BF16), wpp_ref[...])
    if final:
        ms = jnp.mean(h * h, axis=-1, keepdims=True)
        h = h * lax.rsqrt(ms + EPS) * fg_ref[...]
    o_ref[...] = h


def _resident(rows, cols, layer):
    return pl.BlockSpec((None, rows, cols), lambda i: (layer, 0, 0), pipeline_mode=pl.Buffered(1))


def _ffn(h, g, wg, wu, wd, layer):
    T = h.shape[0]
    tm = TOKEN_TILE
    return pl.pallas_call(
        _ffn_kernel,
        out_shape=jax.ShapeDtypeStruct((T, D_MODEL), F32),
        grid=(T // tm,),
        in_specs=[
            pl.BlockSpec((tm, D_MODEL), lambda i: (i, 0)),
            pl.BlockSpec((None, 1, D_MODEL), lambda i: (layer, 0, 0)),
            _resident(D_MODEL, D_FF, layer), _resident(D_MODEL, D_FF, layer), _resident(D_FF, D_MODEL, layer),
        ],
        out_specs=pl.BlockSpec((tm, D_MODEL), lambda i: (i, 0)),
        compiler_params=_params("parallel"),
        name="ffn",
    )(h, g, wg, wu, wd)


def _ffn_ple(h, g, wg, wu, wd, p, pg, wpg, wpp, final_g, layer, final):
    T = h.shape[0]
    tm = TOKEN_TILE
    return pl.pallas_call(
        functools.partial(_ffn_ple_kernel, final=final),
        out_shape=jax.ShapeDtypeStruct((T, D_MODEL), F32),
        grid=(T // tm,),
        in_specs=[
            pl.BlockSpec((tm, D_MODEL), lambda i: (i, 0)),
            pl.BlockSpec((None, 1, D_MODEL), lambda i: (layer, 0, 0)),
            _resident(D_MODEL, D_FF, layer), _resident(D_MODEL, D_FF, layer), _resident(D_FF, D_MODEL, layer),
            pl.BlockSpec((None, tm, D_PLE), lambda i: (layer, i, 0)),
            pl.BlockSpec((None, 1, D_MODEL), lambda i: (layer, 0, 0)),
            _resident(D_MODEL, D_MODEL, layer), _resident(D_PLE, D_MODEL, layer),
            pl.BlockSpec((1, D_MODEL), lambda i: (0, 0)),
        ],
        out_specs=pl.BlockSpec((tm, D_MODEL), lambda i: (i, 0)),
        compiler_params=_params("parallel"),
        name="ffn_ple",
    )(h, g, wg, wu, wd, p, pg, wpg, wpp, final_g)


def _proj_kernel(x_ref, g_ref, win_ref, wgt_ref, za0_ref, za1_ref, za2_ref, zbc_ref, gt_ref, stage):
    tn = PROJ_N_TILE
    tm = TOKEN_TILE
    W = A_GROUP_WIDTH
    za_refs = (za0_ref, za1_ref, za2_ref)
    u = _rms_bf16(x_ref[...], g_ref[...])
    for j in range(3):
        for grp, (_, dil) in enumerate(A_GROUPS):
            col0 = j * A_WIDTH + grp * W
            res = _dot(u, win_ref[:, col0:col0 + W])
            dst = za_refs[grp]
            if dil == 1:
                dst[0, :, j * W:(j + 1) * W] = res.astype(BF16)
                continue
            halves = W // LANE
            for c in range(halves):
                stage[j % 2, grp - 1, c] = res[:, c * LANE:(c + 1) * LANE]
            for r in range(dil):
                for c in range(halves):
                    sub = stage[j % 2, grp - 1, c, pl.ds(r, tm // dil, stride=dil), :]
                    dst[r, :, j * W + c * LANE:j * W + (c + 1) * LANE] = sub.astype(BF16)
    for j in range((N_IN - 3 * A_WIDTH) // tn):
        cols = slice(3 * A_WIDTH + j * tn, 3 * A_WIDTH + (j + 1) * tn)
        zbc_ref[:, j * tn:(j + 1) * tn] = _dot(u, win_ref[:, cols]).astype(BF16)
    for j in range(3 * D_MODEL // tn):
        cols = slice(j * tn, (j + 1) * tn)
        gt_ref[:, cols] = jax.nn.sigmoid(_dot(u, wgt_ref[:, cols])).astype(BF16)


def _rms_proj(h, g, w_in, w_gate, layer, Bn, S):
    T = h.shape[0]
    tm = TOKEN_TILE
    tiles = S // tm
    n_bc = N_IN - 3 * A_WIDTH
    resident = lambda n: pl.BlockSpec((None, D_MODEL, n), lambda i: (layer, 0, 0), pipeline_mode=pl.Buffered(1))
    za_shapes, za_specs = [], []
    for _, dil in A_GROUPS:
        assert tm % (dil * 16) == 0 and S % tm == 0
        za_shapes.append(jax.ShapeDtypeStruct((Bn, dil, S // dil, A_WIDTH), BF16))
        za_specs.append(pl.BlockSpec((None, dil, tm // dil, A_WIDTH), lambda i: (i // tiles, 0, i % tiles, 0)))
    return pl.pallas_call(
        _proj_kernel,
        out_shape=(*za_shapes, jax.ShapeDtypeStruct((T, n_bc), BF16), jax.ShapeDtypeStruct((T, 3 * D_MODEL), BF16)),
        grid=(T // tm,),
        in_specs=[
            pl.BlockSpec((tm, D_MODEL), lambda i: (i, 0)),
            pl.BlockSpec((None, 1, D_MODEL), lambda i: (layer, 0, 0)),
            resident(N_IN),
            resident(3 * D_MODEL),
        ],
        out_specs=(*za_specs, pl.BlockSpec((tm, n_bc), lambda i: (i, 0)), pl.BlockSpec((tm, 3 * D_MODEL), lambda i: (i, 0))),
        scratch_shapes=[pltpu.VMEM((2, len(A_GROUPS) - 1, A_GROUP_WIDTH // LANE, tm, LANE), F32)],
        compiler_params=_params("parallel"),
        name="proj_in",
    )(h, g, w_in, w_gate)


def _t5_bucket_np(rel):
    half = N_BUCKETS // 2
    max_exact = half // 2
    ret = np.where(rel > 0, half, 0)
    n = np.abs(rel)
    nf = np.maximum(n, 1).astype(np.float32)
    large = max_exact + (
        np.log(nf / np.float32(max_exact)) / np.float32(math.log(MAX_DISTANCE / max_exact)) * np.float32(half - max_exact)
    ).astype(np.int32)
    large = np.minimum(large, half - 1)
    return ret + np.where(n < max_exact, n, large)


def _band_attn_kernel(q_ref, kin_ref, vin_ref, bias_ref, o_ref, lse_ref, k_ref, v_ref, *, seq_len):
    C = A_BLOCK
    H = A_HEADS_PER_GROUP
    hd = A_HEAD_DIM
    W = A_GROUP_WIDTH
    n_blocks = seq_len // C
    scale = hd ** -0.5
    assert math.log2(scale) == round(math.log2(scale))

    for src, dst in ((kin_ref, k_ref), (vin_ref, v_ref)):
        dst[0:C, :] = jnp.zeros((C, W), BF16)
        dst[C + seq_len:2 * C + seq_len, :] = jnp.zeros((C, W), BF16)
        dst[C:C + seq_len, :] = src[...]

    lane_head_q = lax.broadcasted_iota(jnp.int32, (C, W), 1) // hd

    def scores(n):
        base = pl.multiple_of(n * C, C)
        qb = q_ref[pl.ds(base, C), :].astype(F32) * scale
        kb = k_ref[pl.ds(base, 3 * C), :]
        qexp = jnp.concatenate([jnp.where(lane_head_q == j, qb, 0.0) for j in range(H)], axis=0).astype(BF16)
        return _dot_nt(qexp, kb)

    def softmax(n, s):
        edge = jnp.where(n == 0, 1, 0) + jnp.where(n == n_blocks - 1, 2, 0)
        s = s + bias_ref[edge]
        m = jnp.max(s, axis=-1, keepdims=True)
        e = jnp.exp(s - m)
        den = jnp.sum(e, axis=-1, keepdims=True)
        return (e / den).astype(BF16), m + jnp.log(den)

    def output(n, p, lse):
        base = pl.multiple_of(n * C, C)
        oall = _dot(p, v_ref[pl.ds(base, 3 * C), :])
        o = jnp.zeros((C, W), F32)
        lse_b = jnp.zeros((C, W), F32)
        for j in range(H):
            sel = lane_head_q == j
            o = jnp.where(sel, oall[j * C:(j + 1) * C, :], o)
            lse_b = jnp.where(sel, lse[j * C:(j + 1) * C, :], lse_b)
        o_ref[pl.ds(base, C), :] = o.astype(o_ref.dtype)
        lse_ref[pl.ds(base, C), :] = lse_b

    nb = math.gcd(n_blocks, A_BLOCKS_PER_STEP)

    def body(i, carry):
        ns = [i * nb + t for t in range(nb)]
        ss = [scores(n) for n in ns]
        pls = [softmax(n, s) for n, s in zip(ns, ss)]
        for n, (p, lse) in zip(ns, pls):
            output(n, p, lse)
        return carry

    lax.fori_loop(0, n_blocks // nb, body, 0)


def _band_attn(za, bias):
    Bn, dil, L, _ = za.shape
    C = A_BLOCK
    W = A_GROUP_WIDTH
    kern = functools.partial(_band_attn_kernel, seq_len=L)
    col = lambda c: pl.BlockSpec((None, None, L, W), lambda i: (i // dil, i % dil, 0, c))
    return pl.pallas_call(
        kern,
        out_shape=(jax.ShapeDtypeStruct((Bn, dil, L, W), BF16), jax.ShapeDtypeStruct((Bn, dil, L, W), F32)),
        grid=(Bn * dil,),
        in_specs=[col(0), col(1), col(2), pl.BlockSpec((4, A_HEADS_PER_GROUP * C, 3 * C), lambda i: (0, 0, 0))],
        out_specs=(col(0), col(0)),
        scratch_shapes=[pltpu.VMEM((L + 2 * C, W), BF16), pltpu.VMEM((L + 2 * C, W), BF16)],
        compiler_params=_params("parallel"),
        name="band_attn",
    )(za, za, za, bias)


def _mixer_a(zas, rel_bias):
    C = A_BLOCK
    outs, lses = [], []
    qi = np.arange(C)[:, None]
    kj = np.arange(3 * C)[None, :]
    off = kj - C - qi
    for g, (window, dil) in enumerate(A_GROUPS):
        n_side = (window // 2) // dil
        L = zas[g].shape[2]
        assert L % C == 0
        bucket = _t5_bucket_np((off * dil).astype(np.int32))
        tbl = rel_bias[:, g * A_HEADS_PER_GROUP:(g + 1) * A_HEADS_PER_GROUP].astype(F32)
        bias = jnp.transpose(tbl[bucket], (2, 0, 1))
        band = np.abs(off) <= n_side
        left, right = kj >= C, kj < 2 * C
        masks = np.stack([band, band & left, band & right, band & left & right])
        bias = jnp.where(masks[:, None], bias[None], -1e30).reshape(4, A_HEADS_PER_GROUP * C, 3 * C)
        o, lse = _band_attn(zas[g], bias)
        outs.append(o)
        lses.append(lse)
    return outs, lses


def _ret_consts():
    C = RET_TILE
    hh = jnp.arange(B_HEADS, dtype=F32)
    lgf = jnp.log1p(-jnp.exp2(-5.0 - hh))[:, None, None]
    lgb = jnp.log1p(-jnp.exp2(-5.5 - hh))[:, None, None]
    i = jnp.arange(C, dtype=F32)[None, :, None]
    j = jnp.arange(C, dtype=F32)[None, None, :]
    rel = i - j
    dfb = (jnp.where(rel >= 0, jnp.exp(jnp.maximum(rel, 0.0) * lgf), 0.0)
           + jnp.where(rel <= 0, jnp.exp(jnp.maximum(-rel, 0.0) * lgb), 0.0))
    ones = jnp.ones((1, 1, LANE), F32)
    rows = jnp.stack([jnp.exp((i + 1.0) * lgf) * ones,
                      jnp.exp((C - i) * lgb) * ones,
                      jnp.exp((C - 1.0 - i) * lgf) * ones,
                      jnp.exp(i * lgb) * ones], axis=1)
    chunk = jnp.concatenate([jnp.exp(C * lgf) * ones, jnp.exp(C * lgb) * ones], axis=1)
    return dfb, rows, chunk


def _ret_kernel(q_ref, k_ref, v_ref, g_ref, cos_ref, sin_ref, d_ref, r_ref, c_ref, gain_ref, o_ref,
                os_, qh_s, kvt_s, st_s, *, seq_len):
    C = RET_TILE
    n_chunks = seq_len // C
    dv = B_HEAD_DIM
    scale = B_HEAD_DIM ** -0.5

    def rot(x, sl):
        return x * cos_ref[sl, :] + pltpu.roll(x, B_HEAD_DIM // 2, 1) * sin_ref[sl, :]

    nb = math.gcd(n_chunks, 4)

    def local(i, carry):
        ns = [i * nb + t for t in range(nb)]
        sls = [pl.ds(pl.multiple_of(n * C, C), C) for n in ns]
        qs = [rot(q_ref[sl, :].astype(F32), sl) for sl in sls]
        ks = [rot(k_ref[sl, :].astype(F32), sl) * scale for sl in sls]
        scs = [_dot_nt(q.astype(BF16), k.astype(BF16)) for q, k in zip(qs, ks)]
        khats = [jnp.concatenate([(k * r_ref[2]).astype(BF16), (k * r_ref[3]).astype(BF16)], axis=1) for k in ks]
        for n, sl, khat in zip(ns, sls, khats):
            kvt_s[n] = _dot_tn(v_ref[sl, :], khat)
        for sl, q in zip(sls, qs):
            qh_s[sl, 0:dv] = (q * r_ref[0]).astype(BF16)
            qh_s[sl, dv:2 * dv] = (q * r_ref[1]).astype(BF16)
        for sl, s in zip(sls, scs):
            os_[sl, :] = _dot((s * d_ref[...]).astype(BF16), v_ref[sl, :])
        return carry

    lax.fori_loop(0, n_chunks // nb, local, 0)

    gcf = c_ref[0:1, :]
    gcb = c_ref[1:2, :]

    def scan(i, carry):
        sf, sb = carry
        nb = n_chunks - 1 - i
        st_s[i, :, 0:dv] = sf.astype(BF16)
        sf = sf * gcf + kvt_s[i, :, 0:dv]
        st_s[nb, :, dv:2 * dv] = sb.astype(BF16)
        sb = sb * gcb + kvt_s[nb, :, dv:2 * dv]
        return sf, sb

    zero = jnp.zeros((dv, dv), F32)
    lax.fori_loop(0, n_chunks, scan, (zero, zero))

    def finish(i, carry):
        ns = [i * nb + t for t in range(nb)]
        sls = [pl.ds(pl.multiple_of(n * C, C), C) for n in ns]
        outs = [os_[sl, :] + _dot_nt(qh_s[sl, :], st_s[n]) for n, sl in zip(ns, sls)]
        for sl, o in zip(sls, outs):
            mu = jnp.mean(o, axis=-1, keepdims=True)
            d = o - mu
            var = jnp.mean(d * d, axis=-1, keepdims=True)
            y = d * lax.rsqrt(var + EPS) * gain_ref[...]
            g = g_ref[sl, :].astype(F32)
            o_ref[sl, :] = (g * jax.nn.sigmoid(g) * y).astype(o_ref.dtype)
        return carry

    lax.fori_loop(0, n_chunks // nb, finish, 0)


def _mixer_b(z3, cos2, sin2, consts, gain):
    Bn, S, _ = z3.shape
    C = RET_TILE
    dfb, rows, chunk = consts

    def zspec(slab):
        return pl.BlockSpec((None, S, LANE), lambda b, h: (b, 0, COL_B + slab * HEAD_COLS + h))

    return pl.pallas_call(
        functools.partial(_ret_kernel, seq_len=S),
        out_shape=jax.ShapeDtypeStruct((Bn, S, B_WIDTH), BF16),
        grid=(Bn, B_HEADS),
        in_specs=[
            zspec(0), zspec(1), zspec(2), zspec(3),
            pl.BlockSpec((S, LANE), lambda b, h: (0, 0)),
            pl.BlockSpec((S, LANE), lambda b, h: (0, 0)),
            pl.BlockSpec((None, C, C), lambda b, h: (h, 0, 0)),
            pl.BlockSpec((None, 4, C, LANE), lambda b, h: (h, 0, 0, 0)),
            pl.BlockSpec((None, 2, LANE), lambda b, h: (h, 0, 0)),
            pl.BlockSpec((1, LANE), lambda b, h: (0, h)),
        ],
        out_specs=pl.BlockSpec((None, S, LANE), lambda b, h: (b, 0, h)),
        scratch_shapes=[
            pltpu.VMEM((S, LANE), F32),
            pltpu.VMEM((S, 2 * LANE), BF16),
            pltpu.VMEM((S // C, LANE, 2 * LANE), F32),
            pltpu.VMEM((S // C, LANE, 2 * LANE), BF16),
        ],
        compiler_params=_params("parallel", "arbitrary"),
        name="retention",
    )(z3, z3, z3, z3, cos2, sin2, dfb, rows, chunk, gain)


def _split3(x):
    hi = x.astype(BF16)
    r1 = x - hi.astype(F32)
    mid = r1.astype(BF16)
    lo = (r1 - mid.astype(F32)).astype(BF16)
    return hi, mid, lo


def _tri_cumsum(tri, x):
    hi, mid, lo = _split3(x)
    return _dot(tri, hi) + _dot(tri, mid) + _dot(tri, lo)


def _gate(z, log_lb, log_1mlb):
    t = jnp.log(1.0 + jnp.exp(-jnp.abs(z)))
    ls_pos = jnp.minimum(z, 0.0) - t
    ls_neg = jnp.minimum(-z, 0.0) - t
    c = log_1mlb + ls_pos
    mx = jnp.maximum(log_lb, c)
    logf = mx + jnp.log(1.0 + jnp.exp(-jnp.abs(log_lb - c)))
    key = jnp.exp(log_1mlb + ls_neg)
    return logf, key


def _gla_masks():
    i = np.arange(GLA_TILE)
    same = (i[:, None] // GLA_CHUNK) == (i[None, :] // GLA_CHUNK)
    m = np.stack([same & (i[:, None] >= i[None, :]), same & (i[:, None] <= i[None, :])]).astype(np.float32)
    return jnp.asarray(m, F32), jnp.asarray(m, BF16)


def _gla_kernel(q_ref, zf_ref, zb_ref, v_ref, g_ref, llb_ref, l1m_ref, gain_ref, mask_ref, tri_ref, o_ref,
                os_, qh_s, kvt_s, st_s, gf_s, gb_s, *, seq_len):
    C = GLA_CHUNK
    R = GLA_TILE
    NC = R // C
    n_tiles = seq_len // R
    n_chunks = seq_len // C
    dk = C_HEAD_DIM
    log_lb = llb_ref[...]
    log_1mlb = l1m_ref[...]

    def scaled(q3, key, b, ref_row, last_row):
        b3 = b.reshape(NC, C, dk)
        key3 = key.reshape(NC, C, dk)
        b_ref = b3[:, ref_row:ref_row + 1, :]
        b_last = b3[:, last_row:last_row + 1, :]
        qt = (q3 * jnp.exp(b3 - b_ref)).reshape(R, dk).astype(BF16)
        kt = (key3 * jnp.exp(b_ref - b3)).reshape(R, dk).astype(BF16)
        qh = (q3 * jnp.exp(b3)).reshape(R, dk).astype(BF16)
        kh = (key3 * jnp.exp(b_last - b3)).reshape(R, dk).astype(BF16)
        return qt, kt, qh, kh, jnp.exp(b_last).reshape(NC, dk)

    nt = math.gcd(n_tiles, 2)

    def local(i, carry):
        ts = [i * nt + u for u in range(nt)]
        sls = [pl.ds(pl.multiple_of(t * R, R), R) for t in ts]
        gates_f = [_gate(zf_ref[sl, :].astype(F32), log_lb, log_1mlb) for sl in sls]
        gates_b = [_gate(zb_ref[sl, :].astype(F32), log_lb, log_1mlb) for sl in sls]
        bfs = [_tri_cumsum(tri_ref[0], logf) for logf, _ in gates_f]
        bbs = [_tri_cumsum(tri_ref[1], logf) for logf, _ in gates_b]
        q3s = []
        for sl in sls:
            q = q_ref[sl, :].astype(F32)
            q3s.append((q * jax.nn.sigmoid(q)).reshape(NC, C, dk))
        fwd = [scaled(q3, key, b, C // 2 - 1, C - 1) for q3, (_, key), b in zip(q3s, gates_f, bfs)]
        bwd = [scaled(q3, key, b, C // 2, 0) for q3, (_, key), b in zip(q3s, gates_b, bbs)]
        sc_f = [_dot_nt(f[0], f[1]) for f in fwd]
        sc_b = [_dot_nt(b[0], b[1]) for b in bwd]
        for t, sl, f, b in zip(ts, sls, fwd, bwd):
            v = v_ref[sl, :]
            khat = jnp.concatenate([f[3], b[3]], axis=1)
            cbase = t * NC
            for c in range(NC):
                rows = slice(c * C, (c + 1) * C)
                kvt_s[cbase + c] = _dot_tn(v[rows, :], khat[rows, :])
            qh_s[sl, 0:dk] = f[2]
            qh_s[sl, dk:2 * dk] = b[2]
            gf_s[pl.ds(cbase, NC), :] = f[4]
            gb_s[pl.ds(cbase, NC), :] = b[4]
        for sl, sf, sb in zip(sls, sc_f, sc_b):
            att = jnp.where(mask_ref[0] > 0.5, sf, 0.0) + jnp.where(mask_ref[1] > 0.5, sb, 0.0)
            os_[sl, :] = _dot(att.astype(BF16), v_ref[sl, :])
        return carry

    lax.fori_loop(0, n_tiles // nt, local, 0)

    def scan(i, carry):
        sf, sb = carry
        nb = n_chunks - 1 - i
        st_s[i, :, 0:dk] = sf.astype(BF16)
        sf = sf * gf_s[pl.ds(i, 1), :] + kvt_s[i, :, 0:dk]
        st_s[nb, :, dk:2 * dk] = sb.astype(BF16)
        sb = sb * gb_s[pl.ds(nb, 1), :] + kvt_s[nb, :, dk:2 * dk]
        return sf, sb

    zero = jnp.zeros((dk, dk), F32)
    lax.fori_loop(0, n_chunks, scan, (zero, zero), unroll=2)

    nb = math.gcd(n_tiles, 4)

    def finish(i, carry):
        outs = []
        for t in [i * nb + u for u in range(nb)]:
            base = pl.multiple_of(t * R, R)
            cbase = t * NC
            parts = []
            for c in range(NC):
                rows = pl.ds(base + c * C, C)
                parts.append(os_[rows, :] + _dot_nt(qh_s[rows, :], st_s[cbase + c]))
            outs.append((pl.ds(base, R), jnp.concatenate(parts, axis=0)))
        for sl, o in outs:
            y = o * lax.rsqrt(jnp.mean(o * o, axis=-1, keepdims=True) + EPS) * gain_ref[...]
            g = g_ref[sl, :].astype(F32)
            o_ref[sl, :] = (g * jax.nn.sigmoid(g) * y).astype(o_ref.dtype)
        return carry

    lax.fori_loop(0, n_tiles // nb, finish, 0)


def _mixer_c(z3, log_lb, log_1mlb, gain):
    Bn, S, _ = z3.shape
    n_chunks = S // GLA_CHUNK

    def zspec(slab):
        return pl.BlockSpec((None, S, LANE), lambda b, h: (b, 0, COL_C + slab * HEAD_COLS + h))

    vec = pl.BlockSpec((1, LANE), lambda b, h: (0, h))
    tile_const = pl.BlockSpec((2, GLA_TILE, GLA_TILE), lambda b, h: (0, 0, 0))
    mask_f32, mask_bf16 = _gla_masks()
    return pl.pallas_call(
        functools.partial(_gla_kernel, seq_len=S),
        out_shape=jax.ShapeDtypeStruct((Bn, S, C_WIDTH), BF16),
        grid=(Bn, C_HEADS),
        in_specs=[zspec(0), zspec(1), zspec(2), zspec(3), zspec(4), vec, vec, vec, tile_const, tile_const],
        out_specs=pl.BlockSpec((None, S, LANE), lambda b, h: (b, 0, h)),
        scratch_shapes=[
            pltpu.VMEM((S, LANE), F32),
            pltpu.VMEM((S, 2 * LANE), BF16),
            pltpu.VMEM((n_chunks, LANE, 2 * LANE), F32),
            pltpu.VMEM((n_chunks, LANE, 2 * LANE), BF16),
            pltpu.VMEM((n_chunks, LANE), F32),
            pltpu.VMEM((n_chunks, LANE), F32),
        ],
        compiler_params=_params("parallel", "arbitrary"),
        name="hgrn2",
    )(z3, z3, z3, z3, z3, log_lb, log_1mlb, gain, mask_f32, mask_bf16)


def _merge_kernel(h_ref, o0, o1, o2, l0, l1, l2, b_ref, c_ref, gt_ref, wa_ref, wb_ref, wc_ref, wo_ref, out_ref, stage):
    tm = TOKEN_TILE
    halves = A_GROUP_WIDTH // LANE

    def token_order(ref, slot, dil, c):
        cols = slice(c * LANE, (c + 1) * LANE)
        if dil == 1:
            return ref[0, :, cols].astype(F32)
        for r in range(dil):
            stage[slot, c, pl.ds(r, tm // dil, stride=dil), :] = ref[r, :, cols].astype(F32)
        return stage[slot, c]

    parts = []
    for c in range(halves):
        os_, ls = [], []
        for grp, (o_ref, l_ref) in enumerate(((o0, l0), (o1, l1), (o2, l2))):
            dil = A_GROUPS[grp][1]
            os_.append(token_order(o_ref, 2 * grp, dil, c))
            ls.append(token_order(l_ref, 2 * grp + 1, dil, c))
        mx = jnp.maximum(jnp.maximum(ls[0], ls[1]), ls[2])
        es = [jnp.exp(l - mx) for l in ls]
        den = es[0] + es[1] + es[2]
        parts.append((es[0] * os_[0] + es[1] * os_[1] + es[2] * os_[2]) / den)
    a = jnp.concatenate(parts, axis=1)
    D = D_MODEL
    m = (gt_ref[:, 0:D].astype(F32) * _dot(a.astype(BF16), wa_ref[...])
         + gt_ref[:, D:2 * D].astype(F32) * _dot(b_ref[...], wb_ref[...])
         + gt_ref[:, 2 * D:3 * D].astype(F32) * _dot(c_ref[...], wc_ref[...]))
    out_ref[...] = h_ref[...] + _dot(m.astype(BF16), wo_ref[...])


def _merge(h, outs, lses, b, c, gates, wa, wb, wc, wo, layer, S):
    T = h.shape[0]
    tm = TOKEN_TILE
    tiles = S // tm
    row = lambda w: pl.BlockSpec((tm, w), lambda i: (i, 0))
    wspec = lambda k: pl.BlockSpec((None, k, D_MODEL), lambda i: (layer, 0, 0))
    sub = [pl.BlockSpec((None, dil, tm // dil, A_GROUP_WIDTH), lambda i: (i // tiles, 0, i % tiles, 0))
           for _, dil in A_GROUPS]
    return pl.pallas_call(
        _merge_kernel,
        out_shape=jax.ShapeDtypeStruct((T, D_MODEL), F32),
        grid=(T // tm,),
        in_specs=[row(D_MODEL)] + sub + sub + [row(B_WIDTH), row(C_WIDTH), row(3 * D_MODEL),
                  wspec(A_GROUP_WIDTH), wspec(B_WIDTH), wspec(C_WIDTH), wspec(D_MODEL)],
        out_specs=row(D_MODEL),
        scratch_shapes=[pltpu.VMEM((2 * len(A_GROUPS), A_GROUP_WIDTH // LANE, tm, LANE), F32)],
        compiler_params=_params("parallel"),
        name="merge",
    )(h, *outs, *lses, b, c, gates, wa, wb, wc, wo)


def _rope_tables(S):
    half = B_HEAD_DIM // 2
    inv = ROPE_BASE ** (-jnp.arange(half, dtype=F32) / half)
    ang = jnp.arange(S, dtype=F32)[:, None] * inv[None]
    cos, sin = jnp.cos(ang), jnp.sin(ang)
    return jnp.concatenate([cos, cos], axis=-1), jnp.concatenate([-sin, sin], axis=-1)


def _trunk(x, p, w):
    Bn, S, D = x.shape
    T = Bn * S
    h = x.reshape(T, D)
    p = p.reshape(DEPTH, T, D_PLE)
    cos2, sin2 = _rope_tables(S)
    ret_consts = _ret_consts()
    for l in range(DEPTH):
        h = _ffn(h, w["ffn1_norm"], w["ffn1_w_gate"], w["ffn1_w_up"], w["ffn1_w_down"], l)
        za0, za1, za2, zbc, gates = _rms_proj(h, w["mix_norm"], w["w_in"], w["w_merge_gate"], l, Bn, S)
        z3 = zbc.reshape(Bn, S, N_IN - 3 * A_WIDTH)
        outs, lses = _mixer_a((za0, za1, za2), w["rel_bias"])
        b = _mixer_b(z3, cos2, sin2, ret_consts, w["ret_norm"][l][None]).reshape(T, B_WIDTH)
        c = _mixer_c(z3, w["log_lb"][l][None], w["log_1mlb"][l][None], w["hgrn_norm"][l][None]).reshape(T, C_WIDTH)
        h = _merge(h, outs, lses, b, c, gates, w["w_branch_a"], w["w_branch_b"], w["w_branch_c"], w["w_out"], l, S)
        h = _ffn_ple(h, w["ffn2_norm"], w["ffn2_w_gate"], w["ffn2_w_up"], w["ffn2_w_down"],
                     p, w["ple_norm"], w["w_ple_gate"], w["w_ple_proj"], w["final_norm"], l, l == DEPTH - 1)
    return h.reshape(Bn, S, D)


def kernel(x_prompt, x_sample, p_prompt, p_sample, ffn1_norm, ffn1_w_gate, ffn1_w_up, ffn1_w_down, mix_norm, w_in, rel_bias, ret_norm, hgrn_lower_bound, hgrn_norm, w_branch_a, w_branch_b, w_branch_c, w_merge_gate, w_out, ffn2_norm, ffn2_w_gate, ffn2_w_up, ffn2_w_down, ple_norm, w_ple_gate, w_ple_proj, final_norm):
    lb = jax.nn.softmax(hgrn_lower_bound.astype(F32), axis=0)
    lb = jnp.cumsum(lb, axis=0)
    lbs = lb - lb[0]
    bf = lambda a: a.astype(BF16)
    vec = lambda a: a.astype(F32)[:, None, :]
    w = dict(
        ffn1_norm=vec(ffn1_norm), ffn1_w_gate=bf(ffn1_w_gate), ffn1_w_up=bf(ffn1_w_up), ffn1_w_down=bf(ffn1_w_down),
        mix_norm=vec(mix_norm), w_in=bf(w_in), rel_bias=rel_bias, ret_norm=ret_norm.astype(F32),
        hgrn_norm=hgrn_norm.astype(F32), log_lb=jnp.log(lbs), log_1mlb=jnp.log1p(-lbs),
        w_branch_a=bf(w_branch_a), w_branch_b=bf(w_branch_b), w_branch_c=bf(w_branch_c),
        w_merge_gate=bf(w_merge_gate), w_out=bf(w_out),
        ffn2_norm=vec(ffn2_norm), ffn2_w_gate=bf(ffn2_w_gate), ffn2_w_up=bf(ffn2_w_up), ffn2_w_down=bf(ffn2_w_down),
        ple_norm=vec(ple_norm), w_ple_gate=bf(w_ple_gate), w_ple_proj=bf(w_ple_proj),
        final_norm=final_norm.astype(F32)[None, :],
    )
    return (_trunk(x_prompt, p_prompt, w), _trunk(x_sample, p_sample, w))
```

```python
import functools
import math

import jax
import jax.numpy as jnp
import numpy as np
from jax import lax
from jax.experimental import pallas as pl
from jax.experimental.pallas import tpu as pltpu

F32 = jnp.float32
BF16 = jnp.bfloat16

EPS = 1e-6
D_MODEL = 1024
D_PLE = 256
D_FF = 2816
DEPTH = 4
A_GROUPS = ((128, 1), (512, 4), (2048, 16))
A_HEADS_PER_GROUP = 4
A_HEAD_DIM = 64
A_HEADS = 12
A_GROUP_WIDTH = A_HEADS_PER_GROUP * A_HEAD_DIM
A_WIDTH = 768
A_BLOCK = 64
A_BLOCKS_PER_STEP = 8
N_BUCKETS = 32
MAX_DISTANCE = 1024
B_HEADS = 4
B_HEAD_DIM = 128
B_WIDTH = 512
RET_TILE = 256
ROPE_BASE = 10000.0
C_HEADS = 4
C_HEAD_DIM = 128
C_WIDTH = 512
GLA_CHUNK = 64
GLA_TILE = 256
N_IN = 3 * A_WIDTH + 4 * B_WIDTH + 5 * C_WIDTH
assert A_GROUPS[0][1] == 1 and all(d > 1 for _, d in A_GROUPS[1:])
LANE = 128
COL_B = 0
COL_C = COL_B + 4 * B_WIDTH // LANE
HEAD_COLS = B_WIDTH // LANE
assert B_WIDTH == C_WIDTH
SLAB_BG, SLAB_CQ, SLAB_CF_FWD, SLAB_CF_BWD, SLAB_CG = 3, 4, 5, 6, 8

VMEM_LIMIT_BYTES = 56 * 1024 * 1024
TOKEN_TILE = 512
FFN_F_TILE = 256
PROJ_N_TILE = 256
PROJ_LOOKAHEAD = 2
PROJ_GATE_ROWS = 64
MERGE_N_TILE = 256


def _params(*sem):
    return pltpu.CompilerParams(dimension_semantics=sem, vmem_limit_bytes=VMEM_LIMIT_BYTES)


def _rms_bf16(x, g):
    ms = jnp.mean(x * x, axis=-1, keepdims=True)
    return (x * lax.rsqrt(ms + EPS) * g).astype(BF16)


def _dot(a, b):
    return jnp.dot(a, b, preferred_element_type=F32)


def _dot_nt(a, b):
    return lax.dot_general(a, b, (((1,), (1,)), ((), ())), preferred_element_type=F32)


def _dot_tn(a, b):
    return lax.dot_general(a, b, (((0,), (0,)), ((), ())), preferred_element_type=F32)


def _swiglu_residual(x, g_ref, wg_ref, wu_ref, wd_ref):
    u = _rms_bf16(x, g_ref[...])
    acc = None
    for f in range(D_FF // FFN_F_TILE):
        cols = slice(f * FFN_F_TILE, (f + 1) * FFN_F_TILE)
        a = _dot(u, wg_ref[:, cols])
        b = _dot(u, wu_ref[:, cols])
        hid = (a * jax.nn.sigmoid(a) * b).astype(BF16)
        d = _dot(hid, wd_ref[cols, :])
        acc = d if acc is None else acc + d
    return x + 0.5 * acc


def _ffn_kernel(x_ref, g_ref, wg_ref, wu_ref, wd_ref, o_ref):
    o_ref[...] = _swiglu_residual(x_ref[...], g_ref, wg_ref, wu_ref, wd_ref)


def _ffn_ple_kernel(x_ref, g_ref, wg_ref, wu_ref, wd_ref, p_ref, pg_ref, wpg_ref, wpp_ref, fg_ref, o_ref, *, final):
    h = _swiglu_residual(x_ref[...], g_ref, wg_ref, wu_ref, wd_ref)
    gate = jax.nn.sigmoid(_dot(_rms_bf16(h, pg_ref[...]), wpg_ref[...]))
    h = h + gate * _dot(p_ref[...].astype(BF16), wpp_ref[...])
    if final:
        ms = jnp.mean(h * h, axis=-1, keepdims=True)
        h = h * lax.rsqrt(ms + EPS) * fg_ref[...]
    o_ref[...] = h


def _resident(rows, cols, layer):
    return pl.BlockSpec((None, rows, cols), lambda i: (layer, 0, 0), pipeline_mode=pl.Buffered(1))


def _ffn(h, g, wg, wu, wd, layer):
    T = h.shape[0]
    tm = TOKEN_TILE
    return pl.pallas_call(
        _ffn_kernel,
        out_shape=jax.ShapeDtypeStruct((T, D_MODEL), F32),
        grid=(T // tm,),
        in_specs=[
            pl.BlockSpec((tm, D_MODEL), lambda i: (i, 0)),
            pl.BlockSpec((None, 1, D_MODEL), lambda i: (layer, 0, 0)),
            _resident(D_MODEL, D_FF, layer), _resident(D_MODEL, D_FF, layer), _resident(D_FF, D_MODEL, layer),
        ],
        out_specs=pl.BlockSpec((tm, D_MODEL), lambda i: (i, 0)),
        compiler_params=_params("parallel"),
        name="ffn",
    )(h, g, wg, wu, wd)


def _ffn_ple(h, g, wg, wu, wd, p, pg, wpg, wpp, final_g, layer, final):
    T = h.shape[0]
    tm = TOKEN_TILE
    return pl.pallas_call(
        functools.partial(_ffn_ple_kernel, final=final),
        out_shape=jax.ShapeDtypeStruct((T, D_MODEL), F32),
        grid=(T // tm,),
        in_specs=[
            pl.BlockSpec((tm, D_MODEL), lambda i: (i, 0)),
            pl.BlockSpec((None, 1, D_MODEL), lambda i: (layer, 0, 0)),
            _resident(D_MODEL, D_FF, layer), _resident(D_MODEL, D_FF, layer), _resident(D_FF, D_MODEL, layer),
            pl.BlockSpec((None, tm, D_PLE), lambda i: (layer, i, 0)),
            pl.BlockSpec((None, 1, D_MODEL), lambda i: (layer, 0, 0)),
            _resident(D_MODEL, D_MODEL, layer), _resident(D_PLE, D_MODEL, layer),
            pl.BlockSpec((1, D_MODEL), lambda i: (0, 0)),
        ],
        out_specs=pl.BlockSpec((tm, D_MODEL), lambda i: (i, 0)),
        compiler_params=_params("parallel"),
        name="ffn_ple",
    )(h, g, wg, wu, wd, p, pg, wpg, wpp, final_g)


def _proj_kernel(x_ref, g_ref, win_ref, llb_ref, l1m_ref, za0_ref, za1_ref, za2_ref, zbc_ref, zlog_ref, stage):
    tn = PROJ_N_TILE
    tm = TOKEN_TILE
    W = A_GROUP_WIDTH
    za_refs = (za0_ref, za1_ref, za2_ref)
    u = _rms_bf16(x_ref[...], g_ref[...])

    light, medium, heavy = [], [], []

    def a_epilogue(j, grp, dil):
        def run(res):
            dst = za_refs[grp]
            if dil == 1:
                dst[0, :, j * W:(j + 1) * W] = res.astype(BF16)
                return
            halves = W // LANE
            for c in range(halves):
                stage[j % 2, grp - 1, c] = res[:, c * LANE:(c + 1) * LANE]
            for r in range(dil):
                for c in range(halves):
                    sub = stage[j % 2, grp - 1, c, pl.ds(r, tm // dil, stride=dil), :]
                    dst[r, :, j * W + c * LANE:j * W + (c + 1) * LANE] = sub.astype(BF16)
        return run

    for j in range(3):
        for grp, (_, dil) in enumerate(A_GROUPS):
            col0 = j * A_WIDTH + grp * W
            light.append((slice(col0, col0 + W), a_epilogue(j, grp, dil)))

    def bc_epilogue(j):
        slab, off = divmod(j * tn, B_WIDTH)

        def run(res):
            if slab in (SLAB_BG, SLAB_CQ, SLAB_CG):
                res = res * jax.nn.sigmoid(res)
            elif slab in (SLAB_CF_FWD, SLAB_CF_BWD):
                lcol = (slab - SLAB_CF_FWD) * C_WIDTH + off
                for r0 in range(0, tm, PROJ_GATE_ROWS):
                    rows = slice(r0, r0 + PROJ_GATE_ROWS)
                    logf, key = _gate(res[rows, :], llb_ref[:, off:off + tn], l1m_ref[:, off:off + tn])
                    zlog_ref[rows, lcol:lcol + tn] = logf
                    zbc_ref[rows, j * tn:(j + 1) * tn] = key.astype(BF16)
                return
            zbc_ref[:, j * tn:(j + 1) * tn] = res.astype(BF16)

        kind = heavy if slab in (SLAB_CF_FWD, SLAB_CF_BWD) else medium if slab in (SLAB_BG, SLAB_CQ, SLAB_CG) else light
        return kind, run

    for j in range((N_IN - 3 * A_WIDTH) // tn):
        kind, run = bc_epilogue(j)
        kind.append((slice(3 * A_WIDTH + j * tn, 3 * A_WIDTH + (j + 1) * tn), run))

    order = []
    busy = heavy + medium
    per = -(-len(light) // max(len(busy), 1))
    for t in busy:
        order.append(t)
        order.extend(light[:per])
        light = light[per:]
    order.extend(light)

    pending = []
    for cols, run in order:
        pending.append((_dot(u, win_ref[:, cols]), run))
        if len(pending) > PROJ_LOOKAHEAD:
            res, fn = pending.pop(0)
            fn(res)
    for res, fn in pending:
        fn(res)


def _rms_proj(h, g, w_in, log_lb, log_1mlb, layer, Bn, S):
    T = h.shape[0]
    tm = TOKEN_TILE
    tiles = S // tm
    n_bc = N_IN - 3 * A_WIDTH
    za_shapes, za_specs = [], []
    for _, dil in A_GROUPS:
        assert tm % (dil * 16) == 0 and S % tm == 0
        za_shapes.append(jax.ShapeDtypeStruct((Bn, dil, S // dil, A_WIDTH), BF16))
        za_specs.append(pl.BlockSpec((None, dil, tm // dil, A_WIDTH), lambda i: (i // tiles, 0, i % tiles, 0)))
    return pl.pallas_call(
        _proj_kernel,
        out_shape=(*za_shapes, jax.ShapeDtypeStruct((T, n_bc), BF16), jax.ShapeDtypeStruct((T, 2 * C_WIDTH), F32)),
        grid=(T // tm,),
        in_specs=[
            pl.BlockSpec((tm, D_MODEL), lambda i: (i, 0)),
            pl.BlockSpec((None, 1, D_MODEL), lambda i: (layer, 0, 0)),
            _resident(D_MODEL, N_IN, layer),
            pl.BlockSpec((1, C_WIDTH), lambda i: (0, 0)),
            pl.BlockSpec((1, C_WIDTH), lambda i: (0, 0)),
        ],
        out_specs=(*za_specs, pl.BlockSpec((tm, n_bc), lambda i: (i, 0)), pl.BlockSpec((tm, 2 * C_WIDTH), lambda i: (i, 0))),
        scratch_shapes=[pltpu.VMEM((2, len(A_GROUPS) - 1, A_GROUP_WIDTH // LANE, tm, LANE), F32)],
        compiler_params=_params("parallel"),
        name="proj_in",
    )(h, g, w_in, log_lb, log_1mlb)


def _t5_bucket_np(rel):
    half = N_BUCKETS // 2
    max_exact = half // 2
    ret = np.where(rel > 0, half, 0)
    n = np.abs(rel)
    nf = np.maximum(n, 1).astype(np.float32)
    large = max_exact + (
        np.log(nf / np.float32(max_exact)) / np.float32(math.log(MAX_DISTANCE / max_exact)) * np.float32(half - max_exact)
    ).astype(np.int32)
    large = np.minimum(large, half - 1)
    return ret + np.where(n < max_exact, n, large)


def _band_attn_kernel(q_ref, kin_ref, vin_ref, bias_ref, o_ref, lse_ref, k_ref, v_ref, *, seq_len):
    C = A_BLOCK
    H = A_HEADS_PER_GROUP
    hd = A_HEAD_DIM
    W = A_GROUP_WIDTH
    n_blocks = seq_len // C
    scale = hd ** -0.5
    assert math.log2(scale) == round(math.log2(scale))

    for src, dst in ((kin_ref, k_ref), (vin_ref, v_ref)):
        dst[0:C, :] = jnp.zeros((C, W), BF16)
        dst[C + seq_len:2 * C + seq_len, :] = jnp.zeros((C, W), BF16)
        dst[C:C + seq_len, :] = src[...]

    lane_head_q = lax.broadcasted_iota(jnp.int32, (C, W), 1) // hd

    def scores(n):
        base = pl.multiple_of(n * C, C)
        qb = q_ref[pl.ds(base, C), :].astype(F32) * scale
        kb = k_ref[pl.ds(base, 3 * C), :]
        qexp = jnp.concatenate([jnp.where(lane_head_q == j, qb, 0.0) for j in range(H)], axis=0).astype(BF16)
        return _dot_nt(qexp, kb)

    def softmax(n, s):
        edge = jnp.where(n == 0, 1, 0) + jnp.where(n == n_blocks - 1, 2, 0)
        s = s + bias_ref[edge]
        m = jnp.max(s, axis=-1, keepdims=True)
        e = jnp.exp(s - m)
        den = jnp.sum(e, axis=-1, keepdims=True)
        return (e / den).astype(BF16), m + jnp.log(den)

    def output(n, p, lse):
        base = pl.multiple_of(n * C, C)
        oall = _dot(p, v_ref[pl.ds(base, 3 * C), :])
        o = jnp.zeros((C, W), F32)
        lse_b = jnp.zeros((C, W), F32)
        for j in range(H):
            sel = lane_head_q == j
            o = jnp.where(sel, oall[j * C:(j + 1) * C, :], o)
            lse_b = jnp.where(sel, lse[j * C:(j + 1) * C, :], lse_b)
        o_ref[pl.ds(base, C), :] = o.astype(o_ref.dtype)
        lse_ref[pl.ds(base, C), :] = lse_b

    nb = math.gcd(n_blocks, A_BLOCKS_PER_STEP)

    def body(i, carry):
        ns = [i * nb + t for t in range(nb)]
        ss = [scores(n) for n in ns]
        pls = [softmax(n, s) for n, s in zip(ns, ss)]
        for n, (p, lse) in zip(ns, pls):
            output(n, p, lse)
        return carry

    lax.fori_loop(0, n_blocks // nb, body, 0)


def _band_attn(za, bias):
    Bn, dil, L, _ = za.shape
    C = A_BLOCK
    W = A_GROUP_WIDTH
    kern = functools.partial(_band_attn_kernel, seq_len=L)
    col = lambda c: pl.BlockSpec((None, None, L, W), lambda i: (i // dil, i % dil, 0, c))
    return pl.pallas_call(
        kern,
        out_shape=(jax.ShapeDtypeStruct((Bn, dil, L, W), BF16), jax.ShapeDtypeStruct((Bn, dil, L, W), F32)),
        grid=(Bn * dil,),
        in_specs=[col(0), col(1), col(2), pl.BlockSpec((4, A_HEADS_PER_GROUP * C, 3 * C), lambda i: (0, 0, 0))],
        out_specs=(col(0), col(0)),
        scratch_shapes=[pltpu.VMEM((L + 2 * C, W), BF16), pltpu.VMEM((L + 2 * C, W), BF16)],
        compiler_params=_params("parallel"),
        name="band_attn",
    )(za, za, za, bias)


def _mixer_a(zas, rel_bias):
    C = A_BLOCK
    outs, lses = [], []
    qi = np.arange(C)[:, None]
    kj = np.arange(3 * C)[None, :]
    off = kj - C - qi
    for g, (window, dil) in enumerate(A_GROUPS):
        n_side = (window // 2) // dil
        L = zas[g].shape[2]
        assert L % C == 0
        bucket = _t5_bucket_np((off * dil).astype(np.int32))
        tbl = rel_bias[:, g * A_HEADS_PER_GROUP:(g + 1) * A_HEADS_PER_GROUP].astype(F32)
        bias = jnp.transpose(tbl[bucket], (2, 0, 1))
        band = np.abs(off) <= n_side
        left, right = kj >= C, kj < 2 * C
        masks = np.stack([band, band & left, band & right, band & left & right])
        bias = jnp.where(masks[:, None], bias[None], -1e30).reshape(4, A_HEADS_PER_GROUP * C, 3 * C)
        o, lse = _band_attn(zas[g], bias)
        outs.append(o)
        lses.append(lse)
    return outs, lses


def _ret_consts():
    C = RET_TILE
    hh = jnp.arange(B_HEADS, dtype=F32)
    lgf = jnp.log1p(-jnp.exp2(-5.0 - hh))[:, None, None]
    lgb = jnp.log1p(-jnp.exp2(-5.5 - hh))[:, None, None]
    i = jnp.arange(C, dtype=F32)[None, :, None]
    j = jnp.arange(C, dtype=F32)[None, None, :]
    rel = i - j
    dfb = (jnp.where(rel >= 0, jnp.exp(jnp.maximum(rel, 0.0) * lgf), 0.0)
           + jnp.where(rel <= 0, jnp.exp(jnp.maximum(-rel, 0.0) * lgb), 0.0))
    ones = jnp.ones((1, 1, LANE), F32)
    rows = jnp.stack([jnp.exp((i + 1.0) * lgf) * ones,
                      jnp.exp((C - i) * lgb) * ones,
                      jnp.exp((C - 1.0 - i) * lgf) * ones,
                      jnp.exp(i * lgb) * ones], axis=1)
    chunk = jnp.concatenate([jnp.exp(C * lgf) * ones, jnp.exp(C * lgb) * ones], axis=1)
    return dfb, rows, chunk


def _ret_kernel(q_ref, k_ref, v_ref, g_ref, cos_ref, sin_ref, d_ref, r_ref, c_ref, gain_ref, o_ref,
                os_, qh_s, kvt_s, st_s, *, seq_len):
    C = RET_TILE
    n_chunks = seq_len // C
    dv = B_HEAD_DIM
    scale = B_HEAD_DIM ** -0.5

    def rot(x, sl):
        return x * cos_ref[sl, :] + pltpu.roll(x, B_HEAD_DIM // 2, 1) * sin_ref[sl, :]

    nb = math.gcd(n_chunks, 4)

    def local(i, carry):
        ns = [i * nb + t for t in range(nb)]
        sls = [pl.ds(pl.multiple_of(n * C, C), C) for n in ns]
        qs = [rot(q_ref[sl, :].astype(F32), sl) for sl in sls]
        ks = [rot(k_ref[sl, :].astype(F32), sl) * scale for sl in sls]
        scs = [_dot_nt(q.astype(BF16), k.astype(BF16)) for q, k in zip(qs, ks)]
        khats = [jnp.concatenate([(k * r_ref[2]).astype(BF16), (k * r_ref[3]).astype(BF16)], axis=1) for k in ks]
        for n, sl, khat in zip(ns, sls, khats):
            kvt_s[n] = _dot_tn(v_ref[sl, :], khat)
        for sl, q in zip(sls, qs):
            qh_s[sl, 0:dv] = (q * r_ref[0]).astype(BF16)
            qh_s[sl, dv:2 * dv] = (q * r_ref[1]).astype(BF16)
        for sl, s in zip(sls, scs):
            os_[sl, :] = _dot((s * d_ref[...]).astype(BF16), v_ref[sl, :])
        return carry

    lax.fori_loop(0, n_chunks // nb, local, 0)

    gcf = c_ref[0:1, :]
    gcb = c_ref[1:2, :]

    def scan(i, carry):
        sf, sb = carry
        nb = n_chunks - 1 - i
        st_s[i, :, 0:dv] = sf.astype(BF16)
        sf = sf * gcf + kvt_s[i, :, 0:dv]
        st_s[nb, :, dv:2 * dv] = sb.astype(BF16)
        sb = sb * gcb + kvt_s[nb, :, dv:2 * dv]
        return sf, sb

    zero = jnp.zeros((dv, dv), F32)
    lax.fori_loop(0, n_chunks, scan, (zero, zero))

    def finish(i, carry):
        ns = [i * nb + t for t in range(nb)]
        sls = [pl.ds(pl.multiple_of(n * C, C), C) for n in ns]
        outs = [os_[sl, :] + _dot_nt(qh_s[sl, :], st_s[n]) for n, sl in zip(ns, sls)]
        for sl, o in zip(sls, outs):
            mu = jnp.mean(o, axis=-1, keepdims=True)
            d = o - mu
            var = jnp.mean(d * d, axis=-1, keepdims=True)
            y = d * lax.rsqrt(var + EPS) * gain_ref[...]
            o_ref[sl, :] = (g_ref[sl, :].astype(F32) * y).astype(o_ref.dtype)
        return carry

    lax.fori_loop(0, n_chunks // nb, finish, 0)


def _mixer_b(z3, cos2, sin2, consts, gain):
    Bn, S, _ = z3.shape
    C = RET_TILE
    dfb, rows, chunk = consts

    def zspec(slab):
        return pl.BlockSpec((None, S, LANE), lambda b, h: (b, 0, COL_B + slab * HEAD_COLS + h))

    return pl.pallas_call(
        functools.partial(_ret_kernel, seq_len=S),
        out_shape=jax.ShapeDtypeStruct((Bn, S, B_WIDTH), BF16),
        grid=(Bn, B_HEADS),
        in_specs=[
            zspec(0), zspec(1), zspec(2), zspec(3),
            pl.BlockSpec((S, LANE), lambda b, h: (0, 0)),
            pl.BlockSpec((S, LANE), lambda b, h: (0, 0)),
            pl.BlockSpec((None, C, C), lambda b, h: (h, 0, 0)),
            pl.BlockSpec((None, 4, C, LANE), lambda b, h: (h, 0, 0, 0)),
            pl.BlockSpec((None, 2, LANE), lambda b, h: (h, 0, 0)),
            pl.BlockSpec((1, LANE), lambda b, h: (0, h)),
        ],
        out_specs=pl.BlockSpec((None, S, LANE), lambda b, h: (b, 0, h)),
        scratch_shapes=[
            pltpu.VMEM((S, LANE), F32),
            pltpu.VMEM((S, 2 * LANE), BF16),
            pltpu.VMEM((S // C, LANE, 2 * LANE), F32),
            pltpu.VMEM((S // C, LANE, 2 * LANE), BF16),
        ],
        compiler_params=_params("parallel", "arbitrary"),
        name="retention",
    )(z3, z3, z3, z3, cos2, sin2, dfb, rows, chunk, gain)


def _split3(x):
    hi = x.astype(BF16)
    r1 = x - hi.astype(F32)
    mid = r1.astype(BF16)
    lo = (r1 - mid.astype(F32)).astype(BF16)
    return hi, mid, lo


def _tri_cumsum(tri, x):
    hi, mid, lo = _split3(x)
    return _dot(tri, hi) + _dot(tri, mid) + _dot(tri, lo)


def _gate(z, log_lb, log_1mlb):
    t = jnp.log(1.0 + jnp.exp(-jnp.abs(z)))
    ls_pos = jnp.minimum(z, 0.0) - t
    ls_neg = jnp.minimum(-z, 0.0) - t
    c = log_1mlb + ls_pos
    mx = jnp.maximum(log_lb, c)
    logf = mx + jnp.log(1.0 + jnp.exp(-jnp.abs(log_lb - c)))
    key = jnp.exp(log_1mlb + ls_neg)
    return logf, key


def _gla_masks():
    i = np.arange(GLA_TILE)
    same = (i[:, None] // GLA_CHUNK) == (i[None, :] // GLA_CHUNK)
    m = np.stack([same & (i[:, None] >= i[None, :]), same & (i[:, None] <= i[None, :])]).astype(np.float32)
    return jnp.asarray(m, F32), jnp.asarray(m, BF16)


def _gla_kernel(q_ref, kf_ref, kb_ref, v_ref, g_ref, lf_ref, lb_ref, gain_ref, mask_ref, tri_ref, o_ref,
                os_, qh_s, kvt_s, st_s, gf_s, gb_s, *, seq_len):
    C = GLA_CHUNK
    R = GLA_TILE
    NC = R // C
    n_tiles = seq_len // R
    n_chunks = seq_len // C
    dk = C_HEAD_DIM

    def scaled(q3, key, b, ref_row, last_row):
        b3 = b.reshape(NC, C, dk)
        key3 = key.reshape(NC, C, dk)
        b_ref = b3[:, ref_row:ref_row + 1, :]
        b_last = b3[:, last_row:last_row + 1, :]
        qt = (q3 * jnp.exp(b3 - b_ref)).reshape(R, dk).astype(BF16)
        kt = (key3 * jnp.exp(b_ref - b3)).reshape(R, dk).astype(BF16)
        qh = (q3 * jnp.exp(b3)).reshape(R, dk).astype(BF16)
        kh = (key3 * jnp.exp(b_last - b3)).reshape(R, dk).astype(BF16)
        return qt, kt, qh, kh, jnp.exp(b_last).reshape(NC, dk)

    nt = math.gcd(n_tiles, 2)

    def local(i, carry):
        ts = [i * nt + u for u in range(nt)]
        sls = [pl.ds(pl.multiple_of(t * R, R), R) for t in ts]
        bfs = [_tri_cumsum(tri_ref[0], lf_ref[sl, :]) for sl in sls]
        bbs = [_tri_cumsum(tri_ref[1], lb_ref[sl, :]) for sl in sls]
        q3s = [q_ref[sl, :].astype(F32).reshape(NC, C, dk) for sl in sls]
        fwd = [scaled(q3, kf_ref[sl, :].astype(F32), b, C // 2 - 1, C - 1) for q3, sl, b in zip(q3s, sls, bfs)]
        bwd = [scaled(q3, kb_ref[sl, :].astype(F32), b, C // 2, 0) for q3, sl, b in zip(q3s, sls, bbs)]
        sc_f = [_dot_nt(f[0], f[1]) for f in fwd]
        sc_b = [_dot_nt(b[0], b[1]) for b in bwd]
        for t, sl, f, b in zip(ts, sls, fwd, bwd):
            v = v_ref[sl, :]
            khat = jnp.concatenate([f[3], b[3]], axis=1)
            cbase = t * NC
            for c in range(NC):
                rows = slice(c * C, (c + 1) * C)
                kvt_s[cbase + c] = _dot_tn(v[rows, :], khat[rows, :])
            qh_s[sl, 0:dk] = f[2]
            qh_s[sl, dk:2 * dk] = b[2]
            gf_s[pl.ds(cbase, NC), :] = f[4]
            gb_s[pl.ds(cbase, NC), :] = b[4]
        for sl, sf, sb in zip(sls, sc_f, sc_b):
            att = jnp.where(mask_ref[0] > 0.5, sf, 0.0) + jnp.where(mask_ref[1] > 0.5, sb, 0.0)
            os_[sl, :] = _dot(att.astype(BF16), v_ref[sl, :])
        return carry

    lax.fori_loop(0, n_tiles // nt, local, 0)

    def scan(i, carry):
        sf, sb = carry
        nb = n_chunks - 1 - i
        st_s[i, :, 0:dk] = sf.astype(BF16)
        sf = sf * gf_s[pl.ds(i, 1), :] + kvt_s[i, :, 0:dk]
        st_s[nb, :, dk:2 * dk] = sb.astype(BF16)
        sb = sb * gb_s[pl.ds(nb, 1), :] + kvt_s[nb, :, dk:2 * dk]
        return sf, sb

    zero = jnp.zeros((dk, dk), F32)
    lax.fori_loop(0, n_chunks, scan, (zero, zero), unroll=2)

    nb = math.gcd(n_tiles, 4)

    def finish(i, carry):
        outs = []
        for t in [i * nb + u for u in range(nb)]:
            base = pl.multiple_of(t * R, R)
            cbase = t * NC
            parts = []
            for c in range(NC):
                rows = pl.ds(base + c * C, C)
                parts.append(os_[rows, :] + _dot_nt(qh_s[rows, :], st_s[cbase + c]))
            outs.append((pl.ds(base, R), jnp.concatenate(parts, axis=0)))
        for sl, o in outs:
            y = o * lax.rsqrt(jnp.mean(o * o, axis=-1, keepdims=True) + EPS) * gain_ref[...]
            o_ref[sl, :] = (g_ref[sl, :].astype(F32) * y).astype(o_ref.dtype)
        return carry

    lax.fori_loop(0, n_tiles // nb, finish, 0)


def _mixer_c(z3, zlog3, gain):
    Bn, S, _ = z3.shape
    n_chunks = S // GLA_CHUNK

    def zspec(slab):
        return pl.BlockSpec((None, S, LANE), lambda b, h: (b, 0, COL_C + slab * HEAD_COLS + h))

    def lspec(direction):
        return pl.BlockSpec((None, S, LANE), lambda b, h: (b, 0, direction * HEAD_COLS + h))

    vec = pl.BlockSpec((1, LANE), lambda b, h: (0, h))
    tile_const = pl.BlockSpec((2, GLA_TILE, GLA_TILE), lambda b, h: (0, 0, 0))
    mask_f32, mask_bf16 = _gla_masks()
    return pl.pallas_call(
        functools.partial(_gla_kernel, seq_len=S),
        out_shape=jax.ShapeDtypeStruct((Bn, S, C_WIDTH), BF16),
        grid=(Bn, C_HEADS),
        in_specs=[zspec(0), zspec(1), zspec(2), zspec(3), zspec(4), lspec(0), lspec(1), vec, tile_const, tile_const],
        out_specs=pl.BlockSpec((None, S, LANE), lambda b, h: (b, 0, h)),
        scratch_shapes=[
            pltpu.VMEM((S, LANE), F32),
            pltpu.VMEM((S, 2 * LANE), BF16),
            pltpu.VMEM((n_chunks, LANE, 2 * LANE), F32),
            pltpu.VMEM((n_chunks, LANE, 2 * LANE), BF16),
            pltpu.VMEM((n_chunks, LANE), F32),
            pltpu.VMEM((n_chunks, LANE), F32),
        ],
        compiler_params=_params("parallel", "arbitrary"),
        name="hgrn2",
    )(z3, z3, z3, z3, z3, zlog3, zlog3, gain, mask_f32, mask_bf16)


def _merge_kernel(h_ref, g_ref, o0, o1, o2, l0, l1, l2, b_ref, c_ref, wgt_ref, wa_ref, wb_ref, wc_ref, wo_ref,
                  out_ref, stage):
    tm = TOKEN_TILE
    halves = A_GROUP_WIDTH // LANE

    def token_order(ref, slot, dil, c):
        cols = slice(c * LANE, (c + 1) * LANE)
        if dil == 1:
            return ref[0, :, cols].astype(F32)
        for r in range(dil):
            stage[slot, c, pl.ds(r, tm // dil, stride=dil), :] = ref[r, :, cols].astype(F32)
        return stage[slot, c]

    parts = []
    for c in range(halves):
        os_, ls = [], []
        for grp, (o_ref, l_ref) in enumerate(((o0, l0), (o1, l1), (o2, l2))):
            dil = A_GROUPS[grp][1]
            os_.append(token_order(o_ref, 2 * grp, dil, c))
            ls.append(token_order(l_ref, 2 * grp + 1, dil, c))
        mx = jnp.maximum(jnp.maximum(ls[0], ls[1]), ls[2])
        es = [jnp.exp(l - mx) for l in ls]
        den = es[0] + es[1] + es[2]
        parts.append((es[0] * os_[0] + es[1] * os_[1] + es[2] * os_[2]) / den)
    a = jnp.concatenate(parts, axis=1).astype(BF16)
    b = b_ref[...]
    c = c_ref[...]
    h = h_ref[...]
    u = _rms_bf16(h, g_ref[...])
    D = D_MODEL
    ms = []
    for j in range(D // MERGE_N_TILE):
        cols = slice(j * MERGE_N_TILE, (j + 1) * MERGE_N_TILE)
        gates = [jax.nn.sigmoid(_dot(u, wgt_ref[:, k * D + j * MERGE_N_TILE:k * D + (j + 1) * MERGE_N_TILE]))
                 for k in range(3)]
        m = (gates[0] * _dot(a, wa_ref[:, cols]) + gates[1] * _dot(b, wb_ref[:, cols])
             + gates[2] * _dot(c, wc_ref[:, cols]))
        ms.append(m.astype(BF16))
    out_ref[...] = h + _dot(jnp.concatenate(ms, axis=1), wo_ref[...])


def _merge(h, g, outs, lses, b, c, w_gate, wa, wb, wc, wo, layer, S):
    T = h.shape[0]
    tm = TOKEN_TILE
    tiles = S // tm
    row = lambda w: pl.BlockSpec((tm, w), lambda i: (i, 0))
    sub = [pl.BlockSpec((None, dil, tm // dil, A_GROUP_WIDTH), lambda i: (i // tiles, 0, i % tiles, 0))
           for _, dil in A_GROUPS]
    return pl.pallas_call(
        _merge_kernel,
        out_shape=jax.ShapeDtypeStruct((T, D_MODEL), F32),
        grid=(T // tm,),
        in_specs=[row(D_MODEL), pl.BlockSpec((None, 1, D_MODEL), lambda i: (layer, 0, 0))] + sub + sub
                 + [row(B_WIDTH), row(C_WIDTH), _resident(D_MODEL, 3 * D_MODEL, layer),
                    _resident(A_GROUP_WIDTH, D_MODEL, layer), _resident(B_WIDTH, D_MODEL, layer),
                    _resident(C_WIDTH, D_MODEL, layer), _resident(D_MODEL, D_MODEL, layer)],
        out_specs=row(D_MODEL),
        scratch_shapes=[pltpu.VMEM((2 * len(A_GROUPS), A_GROUP_WIDTH // LANE, tm, LANE), F32)],
        compiler_params=_params("parallel"),
        name="merge",
    )(h, g, *outs, *lses, b, c, w_gate, wa, wb, wc, wo)


def _rope_tables(S):
    half = B_HEAD_DIM // 2
    inv = ROPE_BASE ** (-jnp.arange(half, dtype=F32) / half)
    ang = jnp.arange(S, dtype=F32)[:, None] * inv[None]
    cos, sin = jnp.cos(ang), jnp.sin(ang)
    return jnp.concatenate([cos, cos], axis=-1), jnp.concatenate([-sin, sin], axis=-1)


def _trunk(x, p, w):
    Bn, S, D = x.shape
    T = Bn * S
    h = x.reshape(T, D)
    p = p.reshape(DEPTH, T, D_PLE)
    cos2, sin2 = _rope_tables(S)
    ret_consts = _ret_consts()
    for l in range(DEPTH):
        h = _ffn(h, w["ffn1_norm"], w["ffn1_w_gate"], w["ffn1_w_up"], w["ffn1_w_down"], l)
        za0, za1, za2, zbc, zlog = _rms_proj(h, w["mix_norm"], w["w_in"], w["log_lb"][l][None],
                                             w["log_1mlb"][l][None], l, Bn, S)
        z3 = zbc.reshape(Bn, S, N_IN - 3 * A_WIDTH)
        outs, lses = _mixer_a((za0, za1, za2), w["rel_bias"])
        b = _mixer_b(z3, cos2, sin2, ret_consts, w["ret_norm"][l][None]).reshape(T, B_WIDTH)
        c = _mixer_c(z3, zlog.reshape(Bn, S, 2 * C_WIDTH), w["hgrn_norm"][l][None]).reshape(T, C_WIDTH)
        h = _merge(h, w["mix_norm"], outs, lses, b, c, w["w_merge_gate"],
                   w["w_branch_a"], w["w_branch_b"], w["w_branch_c"], w["w_out"], l, S)
        h = _ffn_ple(h, w["ffn2_norm"], w["ffn2_w_gate"], w["ffn2_w_up"], w["ffn2_w_down"],
                     p, w["ple_norm"], w["w_ple_gate"], w["w_ple_proj"], w["final_norm"], l, l == DEPTH - 1)
    return h.reshape(Bn, S, D)


def kernel(x_prompt, x_sample, p_prompt, p_sample, ffn1_norm, ffn1_w_gate, ffn1_w_up, ffn1_w_down, mix_norm, w_in, rel_bias, ret_norm, hgrn_lower_bound, hgrn_norm, w_branch_a, w_branch_b, w_branch_c, w_merge_gate, w_out, ffn2_norm, ffn2_w_gate, ffn2_w_up, ffn2_w_down, ple_norm, w_ple_gate, w_ple_proj, final_norm):
    lb = jax.nn.softmax(hgrn_lower_bound.astype(F32), axis=0)
    lb = jnp.cumsum(lb, axis=0)
    lbs = lb - lb[0]
    bf = lambda a: a.astype(BF16)
    vec = lambda a: a.astype(F32)[:, None, :]
    w = dict(
        ffn1_norm=vec(ffn1_norm), ffn1_w_gate=bf(ffn1_w_gate), ffn1_w_up=bf(ffn1_w_up), ffn1_w_down=bf(ffn1_w_down),
        mix_norm=vec(mix_norm), w_in=bf(w_in), rel_bias=rel_bias, ret_norm=ret_norm.astype(F32),
        hgrn_norm=hgrn_norm.astype(F32), log_lb=jnp.log(lbs), log_1mlb=jnp.log1p(-lbs),
        w_branch_a=bf(w_branch_a), w_branch_b=bf(w_branch_b), w_branch_c=bf(w_branch_c),
        w_merge_gate=bf(w_merge_gate), w_out=bf(w_out),
        ffn2_norm=vec(ffn2_norm), ffn2_w_gate=bf(ffn2_w_gate), ffn2_w_up=bf(ffn2_w_up), ffn2_w_down=bf(ffn2_w_down),
        ple_norm=vec(ple_norm), w_ple_gate=bf(w_ple_gate), w_ple_proj=bf(w_ple_proj),
        final_norm=final_norm.astype(F32)[None, :],
    )
    return (_trunk(x_prompt, p_prompt, w), _trunk(x_sample, p_sample, w))
```

```python
import functools
import math

import jax
import jax.numpy as jnp
import numpy as np
from jax import lax
from jax.experimental import pallas as pl
from jax.experimental.pallas import tpu as pltpu

F32 = jnp.float32
BF16 = jnp.bfloat16

EPS = 1e-6
D_MODEL = 1024
D_PLE = 256
D_FF = 2816
DEPTH = 4
A_GROUPS = ((128, 1), (512, 4), (2048, 16))
A_HEADS_PER_GROUP = 4
A_HEAD_DIM = 64
A_HEADS = 12
A_GROUP_WIDTH = A_HEADS_PER_GROUP * A_HEAD_DIM
A_WIDTH = 768
A_BLOCK = 64
A_SCALE = A_HEAD_DIM ** -0.5
assert math.log2(A_SCALE) == round(math.log2(A_SCALE))
A_BLOCKS_PER_STEP = 8
N_BUCKETS = 32
MAX_DISTANCE = 1024
B_HEADS = 4
B_HEAD_DIM = 128
B_WIDTH = 512
RET_TILE = 256
ROPE_BASE = 10000.0
C_HEADS = 4
C_HEAD_DIM = 128
C_WIDTH = 512
GLA_CHUNK = 64
GLA_TILE = 256
N_IN = 3 * A_WIDTH + 4 * B_WIDTH + 5 * C_WIDTH
assert A_GROUPS[0][1] == 1 and all(d > 1 for _, d in A_GROUPS[1:])
LANE = 128
COL_B = 0
COL_C = COL_B + 4 * B_WIDTH // LANE
HEAD_COLS = B_WIDTH // LANE
assert B_WIDTH == C_WIDTH
SLAB_BG, SLAB_CQ, SLAB_CF_FWD, SLAB_CF_BWD, SLAB_CG = 3, 4, 5, 6, 8

VMEM_LIMIT_BYTES = 56 * 1024 * 1024
TOKEN_TILE = 512
FFN_F_TILE = 256
PROJ_N_TILE = 256
PROJ_LOOKAHEAD = 2
PROJ_GATE_ROWS = 64
MERGE_N_TILE = 256


def _params(*sem):
    return pltpu.CompilerParams(dimension_semantics=sem, vmem_limit_bytes=VMEM_LIMIT_BYTES)


def _rms_bf16(x, g):
    ms = jnp.mean(x * x, axis=-1, keepdims=True)
    return (x * lax.rsqrt(ms + EPS) * g).astype(BF16)


def _dot(a, b):
    return jnp.dot(a, b, preferred_element_type=F32)


def _dot_nt(a, b):
    return lax.dot_general(a, b, (((1,), (1,)), ((), ())), preferred_element_type=F32)


def _dot_tn(a, b):
    return lax.dot_general(a, b, (((0,), (0,)), ((), ())), preferred_element_type=F32)


def _swiglu_residual(x, g_ref, wg_ref, wu_ref, wd_ref):
    u = _rms_bf16(x, g_ref[...])
    acc = None
    for f in range(D_FF // FFN_F_TILE):
        cols = slice(f * FFN_F_TILE, (f + 1) * FFN_F_TILE)
        a = _dot(u, wg_ref[:, cols])
        b = _dot(u, wu_ref[:, cols])
        hid = (a * jax.nn.sigmoid(a) * b).astype(BF16)
        d = _dot(hid, wd_ref[cols, :])
        acc = d if acc is None else acc + d
    return x + 0.5 * acc


def _ffn_kernel(x_ref, g_ref, wg_ref, wu_ref, wd_ref, o_ref):
    o_ref[...] = _swiglu_residual(x_ref[...], g_ref, wg_ref, wu_ref, wd_ref)


def _ffn_ple_kernel(x_ref, g_ref, wg_ref, wu_ref, wd_ref, p_ref, pg_ref, wpg_ref, wpp_ref, fg_ref, o_ref, *, final):
    h = _swiglu_residual(x_ref[...], g_ref, wg_ref, wu_ref, wd_ref)
    gate = jax.nn.sigmoid(_dot(_rms_bf16(h, pg_ref[...]), wpg_ref[...]))
    h = h + gate * _dot(p_ref[...].astype(BF16), wpp_ref[...])
    if final:
        ms = jnp.mean(h * h, axis=-1, keepdims=True)
        h = h * lax.rsqrt(ms + EPS) * fg_ref[...]
    o_ref[...] = h


def _resident(rows, cols, layer):
    return pl.BlockSpec((None, rows, cols), lambda i: (layer, 0, 0), pipeline_mode=pl.Buffered(1))


def _ffn(h, g, wg, wu, wd, layer):
    T = h.shape[0]
    tm = TOKEN_TILE
    return pl.pallas_call(
        _ffn_kernel,
        out_shape=jax.ShapeDtypeStruct((T, D_MODEL), F32),
        grid=(T // tm,),
        in_specs=[
            pl.BlockSpec((tm, D_MODEL), lambda i: (i, 0)),
            pl.BlockSpec((None, 1, D_MODEL), lambda i: (layer, 0, 0)),
            _resident(D_MODEL, D_FF, layer), _resident(D_MODEL, D_FF, layer), _resident(D_FF, D_MODEL, layer),
        ],
        out_specs=pl.BlockSpec((tm, D_MODEL), lambda i: (i, 0)),
        compiler_params=_params("parallel"),
        name="ffn",
    )(h, g, wg, wu, wd)


def _ffn_ple(h, g, wg, wu, wd, p, pg, wpg, wpp, final_g, layer, final):
    T = h.shape[0]
    tm = TOKEN_TILE
    return pl.pallas_call(
        functools.partial(_ffn_ple_kernel, final=final),
        out_shape=jax.ShapeDtypeStruct((T, D_MODEL), F32),
        grid=(T // tm,),
        in_specs=[
            pl.BlockSpec((tm, D_MODEL), lambda i: (i, 0)),
            pl.BlockSpec((None, 1, D_MODEL), lambda i: (layer, 0, 0)),
            _resident(D_MODEL, D_FF, layer), _resident(D_MODEL, D_FF, layer), _resident(D_FF, D_MODEL, layer),
            pl.BlockSpec((None, tm, D_PLE), lambda i: (layer, i, 0)),
            pl.BlockSpec((None, 1, D_MODEL), lambda i: (layer, 0, 0)),
            _resident(D_MODEL, D_MODEL, layer), _resident(D_PLE, D_MODEL, layer),
            pl.BlockSpec((1, D_MODEL), lambda i: (0, 0)),
        ],
        out_specs=pl.BlockSpec((tm, D_MODEL), lambda i: (i, 0)),
        compiler_params=_params("parallel"),
        name="ffn_ple",
    )(h, g, wg, wu, wd, p, pg, wpg, wpp, final_g)


def _proj_kernel(x_ref, g_ref, win_ref, llb_ref, l1m_ref, za0_ref, za1_ref, za2_ref, zbc_ref, zlog_ref, stage,
                 u_scr):
    tn = PROJ_N_TILE
    tm = TOKEN_TILE
    W = A_GROUP_WIDTH
    za_refs = (za0_ref, za1_ref, za2_ref)
    u_scr[...] = _rms_bf16(x_ref[...], g_ref[...])

    light, medium, heavy = [], [], []

    def a_epilogue(j, grp, dil):
        def run(res):
            if j == 0:
                res = res * A_SCALE
            dst = za_refs[grp]
            if dil == 1:
                dst[0, :, j * W:(j + 1) * W] = res.astype(BF16)
                return
            halves = W // LANE
            for c in range(halves):
                stage[j % 2, grp - 1, c] = res[:, c * LANE:(c + 1) * LANE]
            for r in range(dil):
                for c in range(halves):
                    sub = stage[j % 2, grp - 1, c, pl.ds(r, tm // dil, stride=dil), :]
                    dst[r, :, j * W + c * LANE:j * W + (c + 1) * LANE] = sub.astype(BF16)
        return run

    for j in range(3):
        for grp, (_, dil) in enumerate(A_GROUPS):
            col0 = j * A_WIDTH + grp * W
            light.append((slice(col0, col0 + W), a_epilogue(j, grp, dil)))

    def bc_epilogue(j):
        slab, off = divmod(j * tn, B_WIDTH)

        def run(res):
            if slab in (SLAB_BG, SLAB_CQ, SLAB_CG):
                res = res * jax.nn.sigmoid(res)
            elif slab in (SLAB_CF_FWD, SLAB_CF_BWD):
                lcol = (slab - SLAB_CF_FWD) * C_WIDTH + off
                for r0 in range(0, tm, PROJ_GATE_ROWS):
                    rows = slice(r0, r0 + PROJ_GATE_ROWS)
                    logf, key = _gate(res[rows, :], llb_ref[:, off:off + tn], l1m_ref[:, off:off + tn])
                    zlog_ref[rows, lcol:lcol + tn] = logf
                    zbc_ref[rows, j * tn:(j + 1) * tn] = key.astype(BF16)
                return
            zbc_ref[:, j * tn:(j + 1) * tn] = res.astype(BF16)

        kind = heavy if slab in (SLAB_CF_FWD, SLAB_CF_BWD) else medium if slab in (SLAB_BG, SLAB_CQ, SLAB_CG) else light
        return kind, run

    for j in range((N_IN - 3 * A_WIDTH) // tn):
        kind, run = bc_epilogue(j)
        kind.append((slice(3 * A_WIDTH + j * tn, 3 * A_WIDTH + (j + 1) * tn), run))

    order = []
    busy = heavy + medium
    per = -(-len(light) // max(len(busy), 1))
    for t in busy:
        order.append(t)
        order.extend(light[:per])
        light = light[per:]
    order.extend(light)

    pending = []
    for cols, run in order:
        pending.append((_dot(u_scr[...], win_ref[:, cols]), run))
        if len(pending) > PROJ_LOOKAHEAD:
            res, fn = pending.pop(0)
            fn(res)
    for res, fn in pending:
        fn(res)


def _rms_proj(h, g, w_in, log_lb, log_1mlb, layer, Bn, S):
    T = h.shape[0]
    tm = TOKEN_TILE
    tiles = S // tm
    n_bc = N_IN - 3 * A_WIDTH
    za_shapes, za_specs = [], []
    for _, dil in A_GROUPS:
        assert tm % (dil * 16) == 0 and S % tm == 0
        za_shapes.append(jax.ShapeDtypeStruct((Bn, dil, S // dil, A_WIDTH), BF16))
        za_specs.append(pl.BlockSpec((None, dil, tm // dil, A_WIDTH), lambda i: (i // tiles, 0, i % tiles, 0)))
    return pl.pallas_call(
        _proj_kernel,
        out_shape=(*za_shapes, jax.ShapeDtypeStruct((T, n_bc), BF16), jax.ShapeDtypeStruct((T, 2 * C_WIDTH), F32)),
        grid=(T // tm,),
        in_specs=[
            pl.BlockSpec((tm, D_MODEL), lambda i: (i, 0)),
            pl.BlockSpec((None, 1, D_MODEL), lambda i: (layer, 0, 0)),
            _resident(D_MODEL, N_IN, layer),
            pl.BlockSpec((1, C_WIDTH), lambda i: (0, 0)),
            pl.BlockSpec((1, C_WIDTH), lambda i: (0, 0)),
        ],
        out_specs=(*za_specs, pl.BlockSpec((tm, n_bc), lambda i: (i, 0)), pl.BlockSpec((tm, 2 * C_WIDTH), lambda i: (i, 0))),
        scratch_shapes=[pltpu.VMEM((2, len(A_GROUPS) - 1, A_GROUP_WIDTH // LANE, tm, LANE), F32),
                        pltpu.VMEM((tm, D_MODEL), BF16)],
        compiler_params=_params("parallel"),
        name="proj_in",
    )(h, g, w_in, log_lb, log_1mlb)


def _t5_bucket_np(rel):
    half = N_BUCKETS // 2
    max_exact = half // 2
    ret = np.where(rel > 0, half, 0)
    n = np.abs(rel)
    nf = np.maximum(n, 1).astype(np.float32)
    large = max_exact + (
        np.log(nf / np.float32(max_exact)) / np.float32(math.log(MAX_DISTANCE / max_exact)) * np.float32(half - max_exact)
    ).astype(np.int32)
    large = np.minimum(large, half - 1)
    return ret + np.where(n < max_exact, n, large)


def _band_attn_kernel(q_ref, kin_ref, vin_ref, bias_ref, o_ref, lse_ref, k_ref, v_ref, *, seq_len, n_res):
    C = A_BLOCK
    H = A_HEADS_PER_GROUP
    hd = A_HEAD_DIM
    W = A_GROUP_WIDTH
    n_blocks = seq_len // C

    for src, dst in ((kin_ref, k_ref), (vin_ref, v_ref)):
        for r in range(n_res):
            dst[r, 0:C, :] = jnp.zeros((C, W), BF16)
            dst[r, C + seq_len:2 * C + seq_len, :] = jnp.zeros((C, W), BF16)
            dst[r, C:C + seq_len, :] = src[r]

    lane_head_q = lax.broadcasted_iota(jnp.int32, (C, W), 1) // hd

    def scores(r, n):
        base = pl.multiple_of(n * C, C)
        qb = q_ref[r, pl.ds(base, C), :]
        kb = k_ref[r, pl.ds(base, 3 * C), :]
        zero = jnp.zeros_like(qb)
        qexp = jnp.concatenate([jnp.where(lane_head_q == j, qb, zero) for j in range(H)], axis=0)
        return _dot_nt(qexp, kb)

    def softmax(n, s):
        edge = jnp.where(n == 0, 1, 0) + jnp.where(n == n_blocks - 1, 2, 0)
        s = s + bias_ref[edge]
        m = jnp.max(s, axis=-1, keepdims=True)
        e = jnp.exp(s - m)
        den = jnp.sum(e, axis=-1, keepdims=True)
        return (e / den).astype(BF16), m + jnp.log(den)

    def output(r, n, p, lse):
        base = pl.multiple_of(n * C, C)
        oall = _dot(p, v_ref[r, pl.ds(base, 3 * C), :])
        o = jnp.zeros((C, W), F32)
        lse_b = jnp.zeros((C, W), F32)
        for j in range(H):
            sel = lane_head_q == j
            o = jnp.where(sel, oall[j * C:(j + 1) * C, :], o)
            lse_b = jnp.where(sel, lse[j * C:(j + 1) * C, :], lse_b)
        o_ref[r, pl.ds(base, C), :] = o.astype(o_ref.dtype)
        lse_ref[r, pl.ds(base, C), :] = lse_b

    units = n_res * n_blocks
    nb = math.gcd(units, A_BLOCKS_PER_STEP)

    def body(i, carry):
        ts = [i * nb + t for t in range(nb)]
        rns = [(t // n_blocks, t % n_blocks) for t in ts]
        ss = [scores(r, n) for r, n in rns]
        pls = [softmax(n, s) for (_, n), s in zip(rns, ss)]
        for (r, n), (p, lse) in zip(rns, pls):
            output(r, n, p, lse)
        return carry

    lax.fori_loop(0, units // nb, body, 0)


def _band_attn(za, bias):
    Bn, dil, L, _ = za.shape
    C = A_BLOCK
    W = A_GROUP_WIDTH
    n_res = math.gcd(dil, max(1, A_BLOCKS_PER_STEP // (L // C)))
    steps = dil // n_res
    kern = functools.partial(_band_attn_kernel, seq_len=L, n_res=n_res)
    col = lambda c: pl.BlockSpec((None, n_res, L, W), lambda i: (i // steps, i % steps, 0, c))
    return pl.pallas_call(
        kern,
        out_shape=(jax.ShapeDtypeStruct((Bn, dil, L, W), BF16), jax.ShapeDtypeStruct((Bn, dil, L, W), F32)),
        grid=(Bn * steps,),
        in_specs=[col(0), col(1), col(2), pl.BlockSpec((4, A_HEADS_PER_GROUP * C, 3 * C), lambda i: (0, 0, 0))],
        out_specs=(col(0), col(0)),
        scratch_shapes=[pltpu.VMEM((n_res, L + 2 * C, W), BF16), pltpu.VMEM((n_res, L + 2 * C, W), BF16)],
        compiler_params=_params("parallel"),
        name="band_attn",
    )(za, za, za, bias)


def _mixer_a(zas, rel_bias):
    C = A_BLOCK
    outs, lses = [], []
    qi = np.arange(C)[:, None]
    kj = np.arange(3 * C)[None, :]
    off = kj - C - qi
    for g, (window, dil) in enumerate(A_GROUPS):
        n_side = (window // 2) // dil
        L = zas[g].shape[2]
        assert L % C == 0
        bucket = _t5_bucket_np((off * dil).astype(np.int32))
        tbl = rel_bias[:, g * A_HEADS_PER_GROUP:(g + 1) * A_HEADS_PER_GROUP].astype(F32)
        bias = jnp.transpose(tbl[bucket], (2, 0, 1))
        band = np.abs(off) <= n_side
        left, right = kj >= C, kj < 2 * C
        masks = np.stack([band, band & left, band & right, band & left & right])
        bias = jnp.where(masks[:, None], bias[None], -1e30).reshape(4, A_HEADS_PER_GROUP * C, 3 * C)
        o, lse = _band_attn(zas[g], bias)
        outs.append(o)
        lses.append(lse)
    return outs, lses


def _ret_consts():
    C = RET_TILE
    hh = jnp.arange(B_HEADS, dtype=F32)
    lgf = jnp.log1p(-jnp.exp2(-5.0 - hh))[:, None, None]
    lgb = jnp.log1p(-jnp.exp2(-5.5 - hh))[:, None, None]
    i = jnp.arange(C, dtype=F32)[None, :, None]
    j = jnp.arange(C, dtype=F32)[None, None, :]
    rel = i - j
    dfb = (jnp.where(rel >= 0, jnp.exp(jnp.maximum(rel, 0.0) * lgf), 0.0)
           + jnp.where(rel <= 0, jnp.exp(jnp.maximum(-rel, 0.0) * lgb), 0.0))
    ones = jnp.ones((1, 1, LANE), F32)
    rows = jnp.stack([jnp.exp((i + 1.0) * lgf) * ones,
                      jnp.exp((C - i) * lgb) * ones,
                      jnp.exp((C - 1.0 - i) * lgf) * ones,
                      jnp.exp(i * lgb) * ones], axis=1)
    chunk = jnp.concatenate([jnp.exp(C * lgf) * ones, jnp.exp(C * lgb) * ones], axis=1)
    return dfb, rows, chunk


def _ret_kernel(q_ref, k_ref, v_ref, g_ref, cos_ref, sin_ref, d_ref, r_ref, c_ref, gain_ref, o_ref,
                os_, qh_s, kvt_s, st_s, *, seq_len):
    C = RET_TILE
    n_chunks = seq_len // C
    dv = B_HEAD_DIM
    scale = B_HEAD_DIM ** -0.5

    def rot(x, sl):
        return x * cos_ref[sl, :] + pltpu.roll(x, B_HEAD_DIM // 2, 1) * sin_ref[sl, :]

    nb = math.gcd(n_chunks, 4)

    def local(i, carry):
        ns = [i * nb + t for t in range(nb)]
        sls = [pl.ds(pl.multiple_of(n * C, C), C) for n in ns]
        qs = [rot(q_ref[sl, :].astype(F32), sl) for sl in sls]
        ks = [rot(k_ref[sl, :].astype(F32), sl) * scale for sl in sls]
        scs = [_dot_nt(q.astype(BF16), k.astype(BF16)) for q, k in zip(qs, ks)]
        khats = [jnp.concatenate([(k * r_ref[2]).astype(BF16), (k * r_ref[3]).astype(BF16)], axis=1) for k in ks]
        for n, sl, khat in zip(ns, sls, khats):
            kvt_s[n] = _dot_tn(v_ref[sl, :], khat)
        for sl, q in zip(sls, qs):
            qh_s[sl, 0:dv] = (q * r_ref[0]).astype(BF16)
            qh_s[sl, dv:2 * dv] = (q * r_ref[1]).astype(BF16)
        for sl, s in zip(sls, scs):
            os_[sl, :] = _dot((s * d_ref[...]).astype(BF16), v_ref[sl, :])
        return carry

    lax.fori_loop(0, n_chunks // nb, local, 0)

    gcf = c_ref[0:1, :]
    gcb = c_ref[1:2, :]

    def scan(i, carry):
        sf, sb = carry
        nb = n_chunks - 1 - i
        st_s[i, :, 0:dv] = sf.astype(BF16)
        sf = sf * gcf + kvt_s[i, :, 0:dv]
        st_s[nb, :, dv:2 * dv] = sb.astype(BF16)
        sb = sb * gcb + kvt_s[nb, :, dv:2 * dv]
        return sf, sb

    zero = jnp.zeros((dv, dv), F32)
    lax.fori_loop(0, n_chunks, scan, (zero, zero))

    def finish(i, carry):
        ns = [i * nb + t for t in range(nb)]
        sls = [pl.ds(pl.multiple_of(n * C, C), C) for n in ns]
        outs = [os_[sl, :] + _dot_nt(qh_s[sl, :], st_s[n]) for n, sl in zip(ns, sls)]
        for sl, o in zip(sls, outs):
            mu = jnp.mean(o, axis=-1, keepdims=True)
            d = o - mu
            var = jnp.mean(d * d, axis=-1, keepdims=True)
            y = d * lax.rsqrt(var + EPS) * gain_ref[...]
            o_ref[sl, :] = (g_ref[sl, :].astype(F32) * y).astype(o_ref.dtype)
        return carry

    lax.fori_loop(0, n_chunks // nb, finish, 0)


def _mixer_b(z3, cos2, sin2, consts, gain):
    Bn, S, _ = z3.shape
    C = RET_TILE
    dfb, rows, chunk = consts

    def zspec(slab):
        return pl.BlockSpec((None, S, LANE), lambda b, h: (b, 0, COL_B + slab * HEAD_COLS + h))

    return pl.pallas_call(
        functools.partial(_ret_kernel, seq_len=S),
        out_shape=jax.ShapeDtypeStruct((Bn, S, B_WIDTH), BF16),
        grid=(Bn, B_HEADS),
        in_specs=[
            zspec(0), zspec(1), zspec(2), zspec(3),
            pl.BlockSpec((S, LANE), lambda b, h: (0, 0)),
            pl.BlockSpec((S, LANE), lambda b, h: (0, 0)),
            pl.BlockSpec((None, C, C), lambda b, h: (h, 0, 0)),
            pl.BlockSpec((None, 4, C, LANE), lambda b, h: (h, 0, 0, 0)),
            pl.BlockSpec((None, 2, LANE), lambda b, h: (h, 0, 0)),
            pl.BlockSpec((1, LANE), lambda b, h: (0, h)),
        ],
        out_specs=pl.BlockSpec((None, S, LANE), lambda b, h: (b, 0, h)),
        scratch_shapes=[
            pltpu.VMEM((S, LANE), F32),
            pltpu.VMEM((S, 2 * LANE), BF16),
            pltpu.VMEM((S // C, LANE, 2 * LANE), F32),
            pltpu.VMEM((S // C, LANE, 2 * LANE), BF16),
        ],
        compiler_params=_params("parallel", "arbitrary"),
        name="retention",
    )(z3, z3, z3, z3, cos2, sin2, dfb, rows, chunk, gain)


def _split3(x):
    hi = x.astype(BF16)
    r1 = x - hi.astype(F32)
    mid = r1.astype(BF16)
    lo = (r1 - mid.astype(F32)).astype(BF16)
    return hi, mid, lo


def _chunk_cumsums(tril, xf, xb, n_chunks, chunk):
    width = xf.shape[-1]
    parts = jnp.concatenate([*_split3(xf), *_split3(xb)], axis=1)
    cs = _dot(tril, parts)
    pf = cs[:, 0:width] + cs[:, width:2 * width] + cs[:, 2 * width:3 * width]
    pb = cs[:, 3 * width:4 * width] + cs[:, 4 * width:5 * width] + cs[:, 5 * width:6 * width]
    pb3 = pb.reshape(n_chunks, chunk, width)
    sb = pb3[:, chunk - 1:chunk, :] - pb3 + xb.reshape(n_chunks, chunk, width)
    return pf, sb.reshape(n_chunks * chunk, width)


def _gate(z, log_lb, log_1mlb):
    t = jnp.log(1.0 + jnp.exp(-jnp.abs(z)))
    ls_pos = jnp.minimum(z, 0.0) - t
    ls_neg = jnp.minimum(-z, 0.0) - t
    c = log_1mlb + ls_pos
    mx = jnp.maximum(log_lb, c)
    logf = mx + jnp.log(1.0 + jnp.exp(-jnp.abs(log_lb - c)))
    key = jnp.exp(log_1mlb + ls_neg)
    return logf, key


def _gla_masks():
    i = np.arange(GLA_TILE)
    same = (i[:, None] // GLA_CHUNK) == (i[None, :] // GLA_CHUNK)
    m = np.stack([same & (i[:, None] >= i[None, :]), same & (i[:, None] <= i[None, :])]).astype(np.float32)
    return jnp.asarray(m, F32), jnp.asarray(m, BF16)


def _gla_kernel(q_ref, kf_ref, kb_ref, v_ref, g_ref, lf_ref, lb_ref, gain_ref, mask_ref, tri_ref, o_ref,
                os_, qh_s, kvt_s, st_s, gf_s, gb_s, *, seq_len):
    C = GLA_CHUNK
    R = GLA_TILE
    NC = R // C
    n_tiles = seq_len // R
    n_chunks = seq_len // C
    dk = C_HEAD_DIM

    def scaled(q3, key, b, ref_row, last_row):
        b3 = b.reshape(NC, C, dk)
        key3 = key.reshape(NC, C, dk)
        b_ref = b3[:, ref_row:ref_row + 1, :]
        b_last = b3[:, last_row:last_row + 1, :]
        qt = (q3 * jnp.exp(b3 - b_ref)).reshape(R, dk).astype(BF16)
        kt = (key3 * jnp.exp(b_ref - b3)).reshape(R, dk).astype(BF16)
        qh = (q3 * jnp.exp(b3)).reshape(R, dk).astype(BF16)
        kh = (key3 * jnp.exp(b_last - b3)).reshape(R, dk).astype(BF16)
        return qt, kt, qh, kh, jnp.exp(b_last).reshape(NC, dk)

    nt = math.gcd(n_tiles, 4)

    def local(i, carry):
        ts = [i * nt + u for u in range(nt)]
        sls = [pl.ds(pl.multiple_of(t * R, R), R) for t in ts]
        sums = [_chunk_cumsums(tri_ref[0], lf_ref[sl, :], lb_ref[sl, :], NC, C) for sl in sls]
        bfs = [s[0] for s in sums]
        bbs = [s[1] for s in sums]
        q3s = [q_ref[sl, :].astype(F32).reshape(NC, C, dk) for sl in sls]
        fwd = [scaled(q3, kf_ref[sl, :].astype(F32), b, C // 2 - 1, C - 1) for q3, sl, b in zip(q3s, sls, bfs)]
        bwd = [scaled(q3, kb_ref[sl, :].astype(F32), b, C // 2, 0) for q3, sl, b in zip(q3s, sls, bbs)]
        sc_f = [_dot_nt(f[0], f[1]) for f in fwd]
        sc_b = [_dot_nt(b[0], b[1]) for b in bwd]
        for t, sl, f, b in zip(ts, sls, fwd, bwd):
            v = v_ref[sl, :]
            khat = jnp.concatenate([f[3], b[3]], axis=1)
            cbase = t * NC
            for c in range(NC):
                rows = slice(c * C, (c + 1) * C)
                kvt_s[cbase + c] = _dot_tn(v[rows, :], khat[rows, :])
            qh_s[sl, 0:dk] = f[2]
            qh_s[sl, dk:2 * dk] = b[2]
            gf_s[pl.ds(cbase, NC), :] = f[4]
            gb_s[pl.ds(cbase, NC), :] = b[4]
        for sl, sf, sb in zip(sls, sc_f, sc_b):
            att = jnp.where(mask_ref[0] > 0.5, sf, 0.0) + jnp.where(mask_ref[1] > 0.5, sb, 0.0)
            os_[sl, :] = _dot(att.astype(BF16), v_ref[sl, :])
        return carry

    lax.fori_loop(0, n_tiles // nt, local, 0)

    def scan(i, carry):
        sf, sb = carry
        nb = n_chunks - 1 - i
        st_s[i, :, 0:dk] = sf.astype(BF16)
        sf = sf * gf_s[pl.ds(i, 1), :] + kvt_s[i, :, 0:dk]
        st_s[nb, :, dk:2 * dk] = sb.astype(BF16)
        sb = sb * gb_s[pl.ds(nb, 1), :] + kvt_s[nb, :, dk:2 * dk]
        return sf, sb

    zero = jnp.zeros((dk, dk), F32)
    lax.fori_loop(0, n_chunks, scan, (zero, zero), unroll=2)

    nb = math.gcd(n_tiles, 4)

    def finish(i, carry):
        outs = []
        for t in [i * nb + u for u in range(nb)]:
            base = pl.multiple_of(t * R, R)
            cbase = t * NC
            parts = []
            for c in range(NC):
                rows = pl.ds(base + c * C, C)
                parts.append(os_[rows, :] + _dot_nt(qh_s[rows, :], st_s[cbase + c]))
            outs.append((pl.ds(base, R), jnp.concatenate(parts, axis=0)))
        for sl, o in outs:
            y = o * lax.rsqrt(jnp.mean(o * o, axis=-1, keepdims=True) + EPS) * gain_ref[...]
            o_ref[sl, :] = (g_ref[sl, :].astype(F32) * y).astype(o_ref.dtype)
        return carry

    lax.fori_loop(0, n_tiles // nb, finish, 0)


def _mixer_c(z3, zlog3, gain):
    Bn, S, _ = z3.shape
    n_chunks = S // GLA_CHUNK

    def zspec(slab):
        return pl.BlockSpec((None, S, LANE), lambda b, h: (b, 0, COL_C + slab * HEAD_COLS + h))

    def lspec(direction):
        return pl.BlockSpec((None, S, LANE), lambda b, h: (b, 0, direction * HEAD_COLS + h))

    vec = pl.BlockSpec((1, LANE), lambda b, h: (0, h))
    tile_const = pl.BlockSpec((2, GLA_TILE, GLA_TILE), lambda b, h: (0, 0, 0))
    mask_f32, mask_bf16 = _gla_masks()
    return pl.pallas_call(
        functools.partial(_gla_kernel, seq_len=S),
        out_shape=jax.ShapeDtypeStruct((Bn, S, C_WIDTH), BF16),
        grid=(Bn, C_HEADS),
        in_specs=[zspec(0), zspec(1), zspec(2), zspec(3), zspec(4), lspec(0), lspec(1), vec, tile_const, tile_const],
        out_specs=pl.BlockSpec((None, S, LANE), lambda b, h: (b, 0, h)),
        scratch_shapes=[
            pltpu.VMEM((S, LANE), F32),
            pltpu.VMEM((S, 2 * LANE), BF16),
            pltpu.VMEM((n_chunks, LANE, 2 * LANE), F32),
            pltpu.VMEM((n_chunks, LANE, 2 * LANE), BF16),
            pltpu.VMEM((n_chunks, LANE), F32),
            pltpu.VMEM((n_chunks, LANE), F32),
        ],
        compiler_params=_params("parallel", "arbitrary"),
        name="hgrn2",
    )(z3, z3, z3, z3, z3, zlog3, zlog3, gain, mask_f32, mask_bf16)


def _merge_kernel(h_ref, g_ref, o0, o1, o2, l0, l1, l2, b_ref, c_ref, wgt_ref, wa_ref, wb_ref, wc_ref, wo_ref,
                  out_ref, stage):
    tm = TOKEN_TILE
    halves = A_GROUP_WIDTH // LANE

    def token_order(ref, slot, dil, c):
        cols = slice(c * LANE, (c + 1) * LANE)
        if dil == 1:
            return ref[0, :, cols].astype(F32)
        for r in range(dil):
            stage[slot, c, pl.ds(r, tm // dil, stride=dil), :] = ref[r, :, cols].astype(F32)
        return stage[slot, c]

    parts = []
    for c in range(halves):
        os_, ls = [], []
        for grp, (o_ref, l_ref) in enumerate(((o0, l0), (o1, l1), (o2, l2))):
            dil = A_GROUPS[grp][1]
            os_.append(token_order(o_ref, 2 * grp, dil, c))
            ls.append(token_order(l_ref, 2 * grp + 1, dil, c))
        mx = jnp.maximum(jnp.maximum(ls[0], ls[1]), ls[2])
        es = [jnp.exp(l - mx) for l in ls]
        den = es[0] + es[1] + es[2]
        parts.append((es[0] * os_[0] + es[1] * os_[1] + es[2] * os_[2]) / den)
    a = jnp.concatenate(parts, axis=1).astype(BF16)
    b = b_ref[...]
    c = c_ref[...]
    h = h_ref[...]
    u = _rms_bf16(h, g_ref[...])
    D = D_MODEL
    ms = []
    for j in range(D // MERGE_N_TILE):
        cols = slice(j * MERGE_N_TILE, (j + 1) * MERGE_N_TILE)
        gates = [jax.nn.sigmoid(_dot(u, wgt_ref[:, k * D + j * MERGE_N_TILE:k * D + (j + 1) * MERGE_N_TILE]))
                 for k in range(3)]
        m = (gates[0] * _dot(a, wa_ref[:, cols]) + gates[1] * _dot(b, wb_ref[:, cols])
             + gates[2] * _dot(c, wc_ref[:, cols]))
        ms.append(m.astype(BF16))
    out_ref[...] = h + _dot(jnp.concatenate(ms, axis=1), wo_ref[...])


def _merge(h, g, outs, lses, b, c, w_gate, wa, wb, wc, wo, layer, S):
    T = h.shape[0]
    tm = TOKEN_TILE
    tiles = S // tm
    row = lambda w: pl.BlockSpec((tm, w), lambda i: (i, 0))
    sub = [pl.BlockSpec((None, dil, tm // dil, A_GROUP_WIDTH), lambda i: (i // tiles, 0, i % tiles, 0))
           for _, dil in A_GROUPS]
    return pl.pallas_call(
        _merge_kernel,
        out_shape=jax.ShapeDtypeStruct((T, D_MODEL), F32),
        grid=(T // tm,),
        in_specs=[row(D_MODEL), pl.BlockSpec((None, 1, D_MODEL), lambda i: (layer, 0, 0))] + sub + sub
                 + [row(B_WIDTH), row(C_WIDTH), _resident(D_MODEL, 3 * D_MODEL, layer),
                    _resident(A_GROUP_WIDTH, D_MODEL, layer), _resident(B_WIDTH, D_MODEL, layer),
                    _resident(C_WIDTH, D_MODEL, layer), _resident(D_MODEL, D_MODEL, layer)],
        out_specs=row(D_MODEL),
        scratch_shapes=[pltpu.VMEM((2 * len(A_GROUPS), A_GROUP_WIDTH // LANE, tm, LANE), F32)],
        compiler_params=_params("parallel"),
        name="merge",
    )(h, g, *outs, *lses, b, c, w_gate, wa, wb, wc, wo)


def _rope_tables(S):
    half = B_HEAD_DIM // 2
    inv = ROPE_BASE ** (-jnp.arange(half, dtype=F32) / half)
    ang = jnp.arange(S, dtype=F32)[:, None] * inv[None]
    cos, sin = jnp.cos(ang), jnp.sin(ang)
    return jnp.concatenate([cos, cos], axis=-1), jnp.concatenate([-sin, sin], axis=-1)


def _trunk(x, p, w):
    Bn, S, D = x.shape
    T = Bn * S
    h = x.reshape(T, D)
    p = p.reshape(DEPTH, T, D_PLE)
    cos2, sin2 = _rope_tables(S)
    ret_consts = _ret_consts()
    for l in range(DEPTH):
        h = _ffn(h, w["ffn1_norm"], w["ffn1_w_gate"], w["ffn1_w_up"], w["ffn1_w_down"], l)
        za0, za1, za2, zbc, zlog = _rms_proj(h, w["mix_norm"], w["w_in"], w["log_lb"][l][None],
                                             w["log_1mlb"][l][None], l, Bn, S)
        z3 = zbc.reshape(Bn, S, N_IN - 3 * A_WIDTH)
        outs, lses = _mixer_a((za0, za1, za2), w["rel_bias"])
        b = _mixer_b(z3, cos2, sin2, ret_consts, w["ret_norm"][l][None]).reshape(T, B_WIDTH)
        c = _mixer_c(z3, zlog.reshape(Bn, S, 2 * C_WIDTH), w["hgrn_norm"][l][None]).reshape(T, C_WIDTH)
        h = _merge(h, w["mix_norm"], outs, lses, b, c, w["w_merge_gate"],
                   w["w_branch_a"], w["w_branch_b"], w["w_branch_c"], w["w_out"], l, S)
        h = _ffn_ple(h, w["ffn2_norm"], w["ffn2_w_gate"], w["ffn2_w_up"], w["ffn2_w_down"],
                     p, w["ple_norm"], w["w_ple_gate"], w["w_ple_proj"], w["final_norm"], l, l == DEPTH - 1)
    return h.reshape(Bn, S, D)


def kernel(x_prompt, x_sample, p_prompt, p_sample, ffn1_norm, ffn1_w_gate, ffn1_w_up, ffn1_w_down, mix_norm, w_in, rel_bias, ret_norm, hgrn_lower_bound, hgrn_norm, w_branch_a, w_branch_b, w_branch_c, w_merge_gate, w_out, ffn2_norm, ffn2_w_gate, ffn2_w_up, ffn2_w_down, ple_norm, w_ple_gate, w_ple_proj, final_norm):
    lb = jax.nn.softmax(hgrn_lower_bound.astype(F32), axis=0)
    lb = jnp.cumsum(lb, axis=0)
    lbs = lb - lb[0]
    bf = lambda a: a.astype(BF16)
    vec = lambda a: a.astype(F32)[:, None, :]
    w = dict(
        ffn1_norm=vec(ffn1_norm), ffn1_w_gate=bf(ffn1_w_gate), ffn1_w_up=bf(ffn1_w_up), ffn1_w_down=bf(ffn1_w_down),
        mix_norm=vec(mix_norm), w_in=bf(w_in), rel_bias=rel_bias, ret_norm=ret_norm.astype(F32),
        hgrn_norm=hgrn_norm.astype(F32), log_lb=jnp.log(lbs), log_1mlb=jnp.log1p(-lbs),
        w_branch_a=bf(w_branch_a), w_branch_b=bf(w_branch_b), w_branch_c=bf(w_branch_c),
        w_merge_gate=bf(w_merge_gate), w_out=bf(w_out),
        ffn2_norm=vec(ffn2_norm), ffn2_w_gate=bf(ffn2_w_gate), ffn2_w_up=bf(ffn2_w_up), ffn2_w_down=bf(ffn2_w_down),
        ple_norm=vec(ple_norm), w_ple_gate=bf(w_ple_gate), w_ple_proj=bf(w_ple_proj),
        final_norm=final_norm.astype(F32)[None, :],
    )
    return (_trunk(x_prompt, p_prompt, w), _trunk(x_sample, p_sample, w))
```

```python
import functools
import math

import jax
import jax.numpy as jnp
import numpy as np
from jax import lax
from jax.experimental import pallas as pl
from jax.experimental.pallas import tpu as pltpu

F32 = jnp.float32
BF16 = jnp.bfloat16

EPS = 1e-6
D_MODEL = 1024
D_PLE = 256
D_FF = 2816
DEPTH = 4
A_GROUPS = ((128, 1), (512, 4), (2048, 16))
A_HEADS_PER_GROUP = 4
A_HEAD_DIM = 64
A_HEADS = 12
A_GROUP_WIDTH = A_HEADS_PER_GROUP * A_HEAD_DIM
A_WIDTH = 768
A_BLOCK = 64
A_SCALE = A_HEAD_DIM ** -0.5
assert math.log2(A_SCALE) == round(math.log2(A_SCALE))
A_BLOCKS_PER_STEP = 8
N_BUCKETS = 32
MAX_DISTANCE = 1024
B_HEADS = 4
B_HEAD_DIM = 128
B_WIDTH = 512
RET_TILE = 256
ROPE_BASE = 10000.0
C_HEADS = 4
C_HEAD_DIM = 128
C_WIDTH = 512
GLA_CHUNK = 64
GLA_TILE = 256
N_IN = 3 * A_WIDTH + 4 * B_WIDTH + 5 * C_WIDTH
assert A_GROUPS[0][1] == 1 and all(d > 1 for _, d in A_GROUPS[1:])
LANE = 128
COL_B = 0
COL_C = COL_B + 4 * B_WIDTH // LANE
HEAD_COLS = B_WIDTH // LANE
assert B_WIDTH == C_WIDTH
SLAB_BG, SLAB_CQ, SLAB_CF_FWD, SLAB_CF_BWD, SLAB_CG = 3, 4, 5, 6, 8

VMEM_LIMIT_BYTES = 56 * 1024 * 1024
TOKEN_TILE = 512
FFN_TOKEN_TILE = 1024
FFN_F_TILE = 256
PROJ_N_TILE = 256
PROJ_LOOKAHEAD = 2
PROJ_GATE_ROWS = 64
MERGE_N_TILE = 256


def _params(*sem):
    return pltpu.CompilerParams(dimension_semantics=sem, vmem_limit_bytes=VMEM_LIMIT_BYTES)


def _rms_bf16(x, g):
    ms = jnp.mean(x * x, axis=-1, keepdims=True)
    return (x * lax.rsqrt(ms + EPS) * g).astype(BF16)


def _dot(a, b):
    return jnp.dot(a, b, preferred_element_type=F32)


def _dot_nt(a, b):
    return lax.dot_general(a, b, (((1,), (1,)), ((), ())), preferred_element_type=F32)


def _dot_tn(a, b):
    return lax.dot_general(a, b, (((0,), (0,)), ((), ())), preferred_element_type=F32)


def _swiglu_residual(x, g_ref, wg_ref, wu_ref, wd_ref):
    u = _rms_bf16(x, g_ref[...])
    acc = None
    for f in range(D_FF // FFN_F_TILE):
        cols = slice(f * FFN_F_TILE, (f + 1) * FFN_F_TILE)
        a = _dot(u, wg_ref[:, cols])
        b = _dot(u, wu_ref[:, cols])
        hid = (a * jax.nn.sigmoid(a) * b).astype(BF16)
        d = _dot(hid, wd_ref[cols, :])
        acc = d if acc is None else acc + d
    return x + 0.5 * acc


def _ffn_kernel(x_ref, g_ref, wg_ref, wu_ref, wd_ref, o_ref):
    o_ref[...] = _swiglu_residual(x_ref[...], g_ref, wg_ref, wu_ref, wd_ref)


def _ffn_ple_kernel(x_ref, g_ref, wg_ref, wu_ref, wd_ref, p_ref, pg_ref, wpg_ref, wpp_ref, fg_ref, o_ref, *, final):
    h = _swiglu_residual(x_ref[...], g_ref, wg_ref, wu_ref, wd_ref)
    gate = jax.nn.sigmoid(_dot(_rms_bf16(h, pg_ref[...]), wpg_ref[...]))
    h = h + gate * _dot(p_ref[...].astype(BF16), wpp_ref[...])
    if final:
        ms = jnp.mean(h * h, axis=-1, keepdims=True)
        h = h * lax.rsqrt(ms + EPS) * fg_ref[...]
    o_ref[...] = h


def _resident(rows, cols, layer):
    return pl.BlockSpec((None, rows, cols), lambda i: (layer, 0, 0), pipeline_mode=pl.Buffered(1))


def _ffn(h, g, wg, wu, wd, layer):
    T = h.shape[0]
    tm = FFN_TOKEN_TILE
    return pl.pallas_call(
        _ffn_kernel,
        out_shape=jax.ShapeDtypeStruct((T, D_MODEL), F32),
        grid=(T // tm,),
        in_specs=[
            pl.BlockSpec((tm, D_MODEL), lambda i: (i, 0)),
            pl.BlockSpec((None, 1, D_MODEL), lambda i: (layer, 0, 0)),
            _resident(D_MODEL, D_FF, layer), _resident(D_MODEL, D_FF, layer), _resident(D_FF, D_MODEL, layer),
        ],
        out_specs=pl.BlockSpec((tm, D_MODEL), lambda i: (i, 0)),
        compiler_params=_params("parallel"),
        name="ffn",
    )(h, g, wg, wu, wd)


def _ffn_ple(h, g, wg, wu, wd, p, pg, wpg, wpp, final_g, layer, final):
    T = h.shape[0]
    tm = FFN_TOKEN_TILE
    return pl.pallas_call(
        functools.partial(_ffn_ple_kernel, final=final),
        out_shape=jax.ShapeDtypeStruct((T, D_MODEL), F32),
        grid=(T // tm,),
        in_specs=[
            pl.BlockSpec((tm, D_MODEL), lambda i: (i, 0)),
            pl.BlockSpec((None, 1, D_MODEL), lambda i: (layer, 0, 0)),
            _resident(D_MODEL, D_FF, layer), _resident(D_MODEL, D_FF, layer), _resident(D_FF, D_MODEL, layer),
            pl.BlockSpec((None, tm, D_PLE), lambda i: (layer, i, 0)),
            pl.BlockSpec((None, 1, D_MODEL), lambda i: (layer, 0, 0)),
            _resident(D_MODEL, D_MODEL, layer), _resident(D_PLE, D_MODEL, layer),
            pl.BlockSpec((1, D_MODEL), lambda i: (0, 0)),
        ],
        out_specs=pl.BlockSpec((tm, D_MODEL), lambda i: (i, 0)),
        compiler_params=_params("parallel"),
        name="ffn_ple",
    )(h, g, wg, wu, wd, p, pg, wpg, wpp, final_g)


def _proj_kernel(x_ref, g_ref, win_ref, llb_ref, l1m_ref, za0_ref, za1_ref, za2_ref, zbc_ref, zlog_ref, stage,
                 u_scr):
    tn = PROJ_N_TILE
    tm = TOKEN_TILE
    W = A_GROUP_WIDTH
    za_refs = (za0_ref, za1_ref, za2_ref)
    u_scr[...] = _rms_bf16(x_ref[...], g_ref[...])

    light, medium, heavy = [], [], []

    def a_epilogue(j, grp, dil):
        def run(res):
            if j == 0:
                res = res * A_SCALE
            dst = za_refs[grp]
            if dil == 1:
                dst[0, :, j * W:(j + 1) * W] = res.astype(BF16)
                return
            halves = W // LANE
            for c in range(halves):
                stage[j % 2, grp - 1, c] = res[:, c * LANE:(c + 1) * LANE]
            for r in range(dil):
                for c in range(halves):
                    sub = stage[j % 2, grp - 1, c, pl.ds(r, tm // dil, stride=dil), :]
                    dst[r, :, j * W + c * LANE:j * W + (c + 1) * LANE] = sub.astype(BF16)
        return run

    for j in range(3):
        for grp, (_, dil) in enumerate(A_GROUPS):
            col0 = j * A_WIDTH + grp * W
            light.append((slice(col0, col0 + W), a_epilogue(j, grp, dil)))

    def bc_epilogue(j):
        slab, off = divmod(j * tn, B_WIDTH)

        def run(res):
            if slab in (SLAB_BG, SLAB_CQ, SLAB_CG):
                res = res * jax.nn.sigmoid(res)
            elif slab in (SLAB_CF_FWD, SLAB_CF_BWD):
                lcol = (slab - SLAB_CF_FWD) * C_WIDTH + off
                for r0 in range(0, tm, PROJ_GATE_ROWS):
                    rows = slice(r0, r0 + PROJ_GATE_ROWS)
                    logf, key = _gate(res[rows, :], llb_ref[:, off:off + tn], l1m_ref[:, off:off + tn])
                    zlog_ref[rows, lcol:lcol + tn] = logf
                    zbc_ref[rows, j * tn:(j + 1) * tn] = key.astype(BF16)
                return
            zbc_ref[:, j * tn:(j + 1) * tn] = res.astype(BF16)

        kind = heavy if slab in (SLAB_CF_FWD, SLAB_CF_BWD) else medium if slab in (SLAB_BG, SLAB_CQ, SLAB_CG) else light
        return kind, run

    for j in range((N_IN - 3 * A_WIDTH) // tn):
        kind, run = bc_epilogue(j)
        kind.append((slice(3 * A_WIDTH + j * tn, 3 * A_WIDTH + (j + 1) * tn), run))

    order = []
    busy = heavy + medium
    per = -(-len(light) // max(len(busy), 1))
    for t in busy:
        order.append(t)
        order.extend(light[:per])
        light = light[per:]
    order.extend(light)

    pending = []
    for cols, run in order:
        pending.append((_dot(u_scr[...], win_ref[:, cols]), run))
        if len(pending) > PROJ_LOOKAHEAD:
            res, fn = pending.pop(0)
            fn(res)
    for res, fn in pending:
        fn(res)


def _rms_proj(h, g, w_in, log_lb, log_1mlb, layer, Bn, S):
    T = h.shape[0]
    tm = TOKEN_TILE
    tiles = S // tm
    n_bc = N_IN - 3 * A_WIDTH
    za_shapes, za_specs = [], []
    for _, dil in A_GROUPS:
        assert tm % (dil * 16) == 0 and S % tm == 0
        za_shapes.append(jax.ShapeDtypeStruct((Bn, dil, S // dil, A_WIDTH), BF16))
        za_specs.append(pl.BlockSpec((None, dil, tm // dil, A_WIDTH), lambda i: (i // tiles, 0, i % tiles, 0)))
    return pl.pallas_call(
        _proj_kernel,
        out_shape=(*za_shapes, jax.ShapeDtypeStruct((T, n_bc), BF16), jax.ShapeDtypeStruct((T, 2 * C_WIDTH), F32)),
        grid=(T // tm,),
        in_specs=[
            pl.BlockSpec((tm, D_MODEL), lambda i: (i, 0)),
            pl.BlockSpec((None, 1, D_MODEL), lambda i: (layer, 0, 0)),
            _resident(D_MODEL, N_IN, layer),
            pl.BlockSpec((1, C_WIDTH), lambda i: (0, 0)),
            pl.BlockSpec((1, C_WIDTH), lambda i: (0, 0)),
        ],
        out_specs=(*za_specs, pl.BlockSpec((tm, n_bc), lambda i: (i, 0)), pl.BlockSpec((tm, 2 * C_WIDTH), lambda i: (i, 0))),
        scratch_shapes=[pltpu.VMEM((2, len(A_GROUPS) - 1, A_GROUP_WIDTH // LANE, tm, LANE), F32),
                        pltpu.VMEM((tm, D_MODEL), BF16)],
        compiler_params=_params("parallel"),
        name="proj_in",
    )(h, g, w_in, log_lb, log_1mlb)


def _t5_bucket_np(rel):
    half = N_BUCKETS // 2
    max_exact = half // 2
    ret = np.where(rel > 0, half, 0)
    n = np.abs(rel)
    nf = np.maximum(n, 1).astype(np.float32)
    large = max_exact + (
        np.log(nf / np.float32(max_exact)) / np.float32(math.log(MAX_DISTANCE / max_exact)) * np.float32(half - max_exact)
    ).astype(np.int32)
    large = np.minimum(large, half - 1)
    return ret + np.where(n < max_exact, n, large)


def _band_attn_kernel(q_ref, kin_ref, vin_ref, bias_ref, o_ref, lse_ref, k_ref, v_ref, *, seq_len, n_res):
    C = A_BLOCK
    H = A_HEADS_PER_GROUP
    hd = A_HEAD_DIM
    W = A_GROUP_WIDTH
    n_blocks = seq_len // C

    for src, dst in ((kin_ref, k_ref), (vin_ref, v_ref)):
        for r in range(n_res):
            dst[r, 0:C, :] = jnp.zeros((C, W), BF16)
            dst[r, C + seq_len:2 * C + seq_len, :] = jnp.zeros((C, W), BF16)
            dst[r, C:C + seq_len, :] = src[r]

    lane_head_q = lax.broadcasted_iota(jnp.int32, (C, W), 1) // hd

    def scores(r, n):
        base = pl.multiple_of(n * C, C)
        qb = q_ref[r, pl.ds(base, C), :]
        kb = k_ref[r, pl.ds(base, 3 * C), :]
        zero = jnp.zeros_like(qb)
        qexp = jnp.concatenate([jnp.where(lane_head_q == j, qb, zero) for j in range(H)], axis=0)
        return _dot_nt(qexp, kb)

    def softmax(n, s):
        edge = jnp.where(n == 0, 1, 0) + jnp.where(n == n_blocks - 1, 2, 0)
        s = s + bias_ref[edge]
        m = jnp.max(s, axis=-1, keepdims=True)
        e = jnp.exp(s - m)
        den = jnp.sum(e, axis=-1, keepdims=True)
        return (e / den).astype(BF16), m + jnp.log(den)

    def output(r, n, p, lse):
        base = pl.multiple_of(n * C, C)
        oall = _dot(p, v_ref[r, pl.ds(base, 3 * C), :])
        o = jnp.zeros((C, W), F32)
        lse_b = jnp.zeros((C, W), F32)
        for j in range(H):
            sel = lane_head_q == j
            o = jnp.where(sel, oall[j * C:(j + 1) * C, :], o)
            lse_b = jnp.where(sel, lse[j * C:(j + 1) * C, :], lse_b)
        o_ref[r, pl.ds(base, C), :] = o.astype(o_ref.dtype)
        lse_ref[r, pl.ds(base, C), :] = lse_b

    units = n_res * n_blocks
    nb = math.gcd(units, A_BLOCKS_PER_STEP)

    def body(i, carry):
        ts = [i * nb + t for t in range(nb)]
        rns = [(t // n_blocks, t % n_blocks) for t in ts]
        ss = [scores(r, n) for r, n in rns]
        pls = [softmax(n, s) for (_, n), s in zip(rns, ss)]
        for (r, n), (p, lse) in zip(rns, pls):
            output(r, n, p, lse)
        return carry

    lax.fori_loop(0, units // nb, body, 0)


def _band_attn(za, bias):
    Bn, dil, L, _ = za.shape
    C = A_BLOCK
    W = A_GROUP_WIDTH
    n_res = math.gcd(dil, max(1, A_BLOCKS_PER_STEP // (L // C)))
    steps = dil // n_res
    kern = functools.partial(_band_attn_kernel, seq_len=L, n_res=n_res)
    col = lambda c: pl.BlockSpec((None, n_res, L, W), lambda i: (i // steps, i % steps, 0, c))
    return pl.pallas_call(
        kern,
        out_shape=(jax.ShapeDtypeStruct((Bn, dil, L, W), BF16), jax.ShapeDtypeStruct((Bn, dil, L, W), F32)),
        grid=(Bn * steps,),
        in_specs=[col(0), col(1), col(2), pl.BlockSpec((4, A_HEADS_PER_GROUP * C, 3 * C), lambda i: (0, 0, 0))],
        out_specs=(col(0), col(0)),
        scratch_shapes=[pltpu.VMEM((n_res, L + 2 * C, W), BF16), pltpu.VMEM((n_res, L + 2 * C, W), BF16)],
        compiler_params=_params("parallel"),
        name="band_attn",
    )(za, za, za, bias)


def _mixer_a(zas, rel_bias):
    C = A_BLOCK
    outs, lses = [], []
    qi = np.arange(C)[:, None]
    kj = np.arange(3 * C)[None, :]
    off = kj - C - qi
    for g, (window, dil) in enumerate(A_GROUPS):
        n_side = (window // 2) // dil
        L = zas[g].shape[2]
        assert L % C == 0
        bucket = _t5_bucket_np((off * dil).astype(np.int32))
        tbl = rel_bias[:, g * A_HEADS_PER_GROUP:(g + 1) * A_HEADS_PER_GROUP].astype(F32)
        onehot = bucket[None] == np.arange(N_BUCKETS)[:, None, None]
        bias = jnp.sum(jnp.where(onehot[:, None], tbl[:, :, None, None], 0.0), axis=0)
        band = np.abs(off) <= n_side
        left, right = kj >= C, kj < 2 * C
        masks = np.stack([band, band & left, band & right, band & left & right])
        bias = jnp.where(masks[:, None], bias[None], -1e30).reshape(4, A_HEADS_PER_GROUP * C, 3 * C)
        o, lse = _band_attn(zas[g], bias)
        outs.append(o)
        lses.append(lse)
    return outs, lses


def _ret_consts():
    C = RET_TILE
    hh = jnp.arange(B_HEADS, dtype=F32)
    lgf = jnp.log1p(-jnp.exp2(-5.0 - hh))[:, None, None]
    lgb = jnp.log1p(-jnp.exp2(-5.5 - hh))[:, None, None]
    i = jnp.arange(C, dtype=F32)[None, :, None]
    j = jnp.arange(C, dtype=F32)[None, None, :]
    rel = i - j
    dfb = (jnp.where(rel >= 0, jnp.exp(jnp.maximum(rel, 0.0) * lgf), 0.0)
           + jnp.where(rel <= 0, jnp.exp(jnp.maximum(-rel, 0.0) * lgb), 0.0))
    ones = jnp.ones((1, 1, LANE), F32)
    rows = jnp.stack([jnp.exp((i + 1.0) * lgf) * ones,
                      jnp.exp((C - i) * lgb) * ones,
                      jnp.exp((C - 1.0 - i) * lgf) * ones,
                      jnp.exp(i * lgb) * ones], axis=1)
    chunk = jnp.concatenate([jnp.exp(C * lgf) * ones, jnp.exp(C * lgb) * ones], axis=1)
    return dfb, rows, chunk


def _ret_kernel(q_ref, k_ref, v_ref, g_ref, cos_ref, sin_ref, d_ref, r_ref, c_ref, gain_ref, o_ref,
                os_, qh_s, kvt_s, st_s, *, seq_len):
    C = RET_TILE
    n_chunks = seq_len // C
    dv = B_HEAD_DIM
    scale = B_HEAD_DIM ** -0.5

    def rot(x, sl):
        return x * cos_ref[sl, :] + pltpu.roll(x, B_HEAD_DIM // 2, 1) * sin_ref[sl, :]

    nb = math.gcd(n_chunks, 4)

    def local(i, carry):
        ns = [i * nb + t for t in range(nb)]
        sls = [pl.ds(pl.multiple_of(n * C, C), C) for n in ns]
        qs = [rot(q_ref[sl, :].astype(F32), sl) for sl in sls]
        ks = [rot(k_ref[sl, :].astype(F32), sl) * scale for sl in sls]
        scs = [_dot_nt(q.astype(BF16), k.astype(BF16)) for q, k in zip(qs, ks)]
        khats = [jnp.concatenate([(k * r_ref[2]).astype(BF16), (k * r_ref[3]).astype(BF16)], axis=1) for k in ks]
        for n, sl, khat in zip(ns, sls, khats):
            kvt_s[n] = _dot_tn(v_ref[sl, :], khat)
        for sl, q in zip(sls, qs):
            qh_s[sl, 0:dv] = (q * r_ref[0]).astype(BF16)
            qh_s[sl, dv:2 * dv] = (q * r_ref[1]).astype(BF16)
        for sl, s in zip(sls, scs):
            os_[sl, :] = _dot((s * d_ref[...]).astype(BF16), v_ref[sl, :])
        return carry

    lax.fori_loop(0, n_chunks // nb, local, 0)

    gcf = c_ref[0:1, :]
    gcb = c_ref[1:2, :]

    def scan(i, carry):
        sf, sb = carry
        nb = n_chunks - 1 - i
        st_s[i, :, 0:dv] = sf.astype(BF16)
        sf = sf * gcf + kvt_s[i, :, 0:dv]
        st_s[nb, :, dv:2 * dv] = sb.astype(BF16)
        sb = sb * gcb + kvt_s[nb, :, dv:2 * dv]
        return sf, sb

    zero = jnp.zeros((dv, dv), F32)
    lax.fori_loop(0, n_chunks, scan, (zero, zero))

    def finish(i, carry):
        ns = [i * nb + t for t in range(nb)]
        sls = [pl.ds(pl.multiple_of(n * C, C), C) for n in ns]
        outs = [os_[sl, :] + _dot_nt(qh_s[sl, :], st_s[n]) for n, sl in zip(ns, sls)]
        for sl, o in zip(sls, outs):
            mu = jnp.mean(o, axis=-1, keepdims=True)
            d = o - mu
            var = jnp.mean(d * d, axis=-1, keepdims=True)
            y = d * lax.rsqrt(var + EPS) * gain_ref[...]
            o_ref[sl, :] = (g_ref[sl, :].astype(F32) * y).astype(o_ref.dtype)
        return carry

    lax.fori_loop(0, n_chunks // nb, finish, 0)


def _mixer_b(z3, cos2, sin2, consts, gain):
    Bn, S, _ = z3.shape
    C = RET_TILE
    dfb, rows, chunk = consts

    def zspec(slab):
        return pl.BlockSpec((None, S, LANE), lambda b, h: (b, 0, COL_B + slab * HEAD_COLS + h))

    return pl.pallas_call(
        functools.partial(_ret_kernel, seq_len=S),
        out_shape=jax.ShapeDtypeStruct((Bn, S, B_WIDTH), BF16),
        grid=(Bn, B_HEADS),
        in_specs=[
            zspec(0), zspec(1), zspec(2), zspec(3),
            pl.BlockSpec((S, LANE), lambda b, h: (0, 0)),
            pl.BlockSpec((S, LANE), lambda b, h: (0, 0)),
            pl.BlockSpec((None, C, C), lambda b, h: (h, 0, 0)),
            pl.BlockSpec((None, 4, C, LANE), lambda b, h: (h, 0, 0, 0)),
            pl.BlockSpec((None, 2, LANE), lambda b, h: (h, 0, 0)),
            pl.BlockSpec((1, LANE), lambda b, h: (0, h)),
        ],
        out_specs=pl.BlockSpec((None, S, LANE), lambda b, h: (b, 0, h)),
        scratch_shapes=[
            pltpu.VMEM((S, LANE), F32),
            pltpu.VMEM((S, 2 * LANE), BF16),
            pltpu.VMEM((S // C, LANE, 2 * LANE), F32),
            pltpu.VMEM((S // C, LANE, 2 * LANE), BF16),
        ],
        compiler_params=_params("parallel", "arbitrary"),
        name="retention",
    )(z3, z3, z3, z3, cos2, sin2, dfb, rows, chunk, gain)


def _split3(x):
    hi = x.astype(BF16)
    r1 = x - hi.astype(F32)
    mid = r1.astype(BF16)
    lo = (r1 - mid.astype(F32)).astype(BF16)
    return hi, mid, lo


def _chunk_cumsums(tril, xf, xb, n_chunks, chunk):
    width = xf.shape[-1]
    parts = jnp.concatenate([*_split3(xf), *_split3(xb)], axis=1)
    cs = _dot(tril, parts)
    pf = cs[:, 0:width] + cs[:, width:2 * width] + cs[:, 2 * width:3 * width]
    pb = cs[:, 3 * width:4 * width] + cs[:, 4 * width:5 * width] + cs[:, 5 * width:6 * width]
    pb3 = pb.reshape(n_chunks, chunk, width)
    sb = pb3[:, chunk - 1:chunk, :] - pb3 + xb.reshape(n_chunks, chunk, width)
    return pf, sb.reshape(n_chunks * chunk, width)


def _gate(z, log_lb, log_1mlb):
    t = jnp.log(1.0 + jnp.exp(-jnp.abs(z)))
    ls_pos = jnp.minimum(z, 0.0) - t
    ls_neg = jnp.minimum(-z, 0.0) - t
    c = log_1mlb + ls_pos
    mx = jnp.maximum(log_lb, c)
    logf = mx + jnp.log(1.0 + jnp.exp(-jnp.abs(log_lb - c)))
    key = jnp.exp(log_1mlb + ls_neg)
    return logf, key


def _gla_masks():
    i = np.arange(GLA_TILE)
    same = (i[:, None] // GLA_CHUNK) == (i[None, :] // GLA_CHUNK)
    m = np.stack([same & (i[:, None] >= i[None, :]), same & (i[:, None] <= i[None, :])]).astype(np.float32)
    return jnp.asarray(m, F32), jnp.asarray(m, BF16)


def _gla_kernel(q_ref, kf_ref, kb_ref, v_ref, g_ref, lf_ref, lb_ref, gain_ref, mask_ref, tri_ref, o_ref,
                os_, qh_s, kvt_s, st_s, gf_s, gb_s, *, seq_len):
    C = GLA_CHUNK
    R = GLA_TILE
    NC = R // C
    n_tiles = seq_len // R
    n_chunks = seq_len // C
    dk = C_HEAD_DIM

    def scaled(q3, key, b, ref_row, last_row):
        b3 = b.reshape(NC, C, dk)
        key3 = key.reshape(NC, C, dk)
        b_ref = b3[:, ref_row:ref_row + 1, :]
        b_last = b3[:, last_row:last_row + 1, :]
        qt = (q3 * jnp.exp(b3 - b_ref)).reshape(R, dk).astype(BF16)
        kt = (key3 * jnp.exp(b_ref - b3)).reshape(R, dk).astype(BF16)
        qh = (q3 * jnp.exp(b3)).reshape(R, dk).astype(BF16)
        kh = (key3 * jnp.exp(b_last - b3)).reshape(R, dk).astype(BF16)
        return qt, kt, qh, kh, jnp.exp(b_last).reshape(NC, dk)

    nt = math.gcd(n_tiles, 4)

    def local(i, carry):
        ts = [i * nt + u for u in range(nt)]
        sls = [pl.ds(pl.multiple_of(t * R, R), R) for t in ts]
        sums = [_chunk_cumsums(tri_ref[0], lf_ref[sl, :], lb_ref[sl, :], NC, C) for sl in sls]
        bfs = [s[0] for s in sums]
        bbs = [s[1] for s in sums]
        q3s = [q_ref[sl, :].astype(F32).reshape(NC, C, dk) for sl in sls]
        fwd = [scaled(q3, kf_ref[sl, :].astype(F32), b, C // 2 - 1, C - 1) for q3, sl, b in zip(q3s, sls, bfs)]
        bwd = [scaled(q3, kb_ref[sl, :].astype(F32), b, C // 2, 0) for q3, sl, b in zip(q3s, sls, bbs)]
        sc_f = [_dot_nt(f[0], f[1]) for f in fwd]
        sc_b = [_dot_nt(b[0], b[1]) for b in bwd]
        for t, sl, f, b in zip(ts, sls, fwd, bwd):
            v = v_ref[sl, :]
            khat = jnp.concatenate([f[3], b[3]], axis=1)
            cbase = t * NC
            for c in range(NC):
                rows = slice(c * C, (c + 1) * C)
                kvt_s[cbase + c] = _dot_tn(v[rows, :], khat[rows, :])
            qh_s[sl, 0:dk] = f[2]
            qh_s[sl, dk:2 * dk] = b[2]
            gf_s[pl.ds(cbase, NC), :] = f[4]
            gb_s[pl.ds(cbase, NC), :] = b[4]
        for sl, sf, sb in zip(sls, sc_f, sc_b):
            att = jnp.where(mask_ref[0] > 0.5, sf, 0.0) + jnp.where(mask_ref[1] > 0.5, sb, 0.0)
            os_[sl, :] = _dot(att.astype(BF16), v_ref[sl, :])
        return carry

    lax.fori_loop(0, n_tiles // nt, local, 0)

    def scan(i, carry):
        sf, sb = carry
        nb = n_chunks - 1 - i
        st_s[i, :, 0:dk] = sf.astype(BF16)
        sf = sf * gf_s[pl.ds(i, 1), :] + kvt_s[i, :, 0:dk]
        st_s[nb, :, dk:2 * dk] = sb.astype(BF16)
        sb = sb * gb_s[pl.ds(nb, 1), :] + kvt_s[nb, :, dk:2 * dk]
        return sf, sb

    zero = jnp.zeros((dk, dk), F32)
    lax.fori_loop(0, n_chunks, scan, (zero, zero), unroll=2)

    nb = math.gcd(n_tiles, 4)

    def finish(i, carry):
        outs = []
        for t in [i * nb + u for u in range(nb)]:
            base = pl.multiple_of(t * R, R)
            cbase = t * NC
            parts = []
            for c in range(NC):
                rows = pl.ds(base + c * C, C)
                parts.append(os_[rows, :] + _dot_nt(qh_s[rows, :], st_s[cbase + c]))
            outs.append((pl.ds(base, R), jnp.concatenate(parts, axis=0)))
        for sl, o in outs:
            y = o * lax.rsqrt(jnp.mean(o * o, axis=-1, keepdims=True) + EPS) * gain_ref[...]
            o_ref[sl, :] = (g_ref[sl, :].astype(F32) * y).astype(o_ref.dtype)
        return carry

    lax.fori_loop(0, n_tiles // nb, finish, 0)


def _mixer_c(z3, zlog3, gain):
    Bn, S, _ = z3.shape
    n_chunks = S // GLA_CHUNK

    def zspec(slab):
        return pl.BlockSpec((None, S, LANE), lambda b, h: (b, 0, COL_C + slab * HEAD_COLS + h))

    def lspec(direction):
        return pl.BlockSpec((None, S, LANE), lambda b, h: (b, 0, direction * HEAD_COLS + h))

    vec = pl.BlockSpec((1, LANE), lambda b, h: (0, h))
    tile_const = pl.BlockSpec((2, GLA_TILE, GLA_TILE), lambda b, h: (0, 0, 0))
    mask_f32, mask_bf16 = _gla_masks()
    return pl.pallas_call(
        functools.partial(_gla_kernel, seq_len=S),
        out_shape=jax.ShapeDtypeStruct((Bn, S, C_WIDTH), BF16),
        grid=(Bn, C_HEADS),
        in_specs=[zspec(0), zspec(1), zspec(2), zspec(3), zspec(4), lspec(0), lspec(1), vec, tile_const, tile_const],
        out_specs=pl.BlockSpec((None, S, LANE), lambda b, h: (b, 0, h)),
        scratch_shapes=[
            pltpu.VMEM((S, LANE), F32),
            pltpu.VMEM((S, 2 * LANE), BF16),
            pltpu.VMEM((n_chunks, LANE, 2 * LANE), F32),
            pltpu.VMEM((n_chunks, LANE, 2 * LANE), BF16),
            pltpu.VMEM((n_chunks, LANE), F32),
            pltpu.VMEM((n_chunks, LANE), F32),
        ],
        compiler_params=_params("parallel", "arbitrary"),
        name="hgrn2",
    )(z3, z3, z3, z3, z3, zlog3, zlog3, gain, mask_f32, mask_bf16)


def _merge_kernel(h_ref, g_ref, o0, o1, o2, l0, l1, l2, b_ref, c_ref, wgt_ref, wa_ref, wb_ref, wc_ref, wo_ref,
                  out_ref, stage):
    tm = TOKEN_TILE
    halves = A_GROUP_WIDTH // LANE

    def token_order(ref, slot, dil, c):
        cols = slice(c * LANE, (c + 1) * LANE)
        if dil == 1:
            return ref[0, :, cols].astype(F32)
        for r in range(dil):
            stage[slot, c, pl.ds(r, tm // dil, stride=dil), :] = ref[r, :, cols].astype(F32)
        return stage[slot, c]

    parts = []
    for c in range(halves):
        os_, ls = [], []
        for grp, (o_ref, l_ref) in enumerate(((o0, l0), (o1, l1), (o2, l2))):
            dil = A_GROUPS[grp][1]
            os_.append(token_order(o_ref, 2 * grp, dil, c))
            ls.append(token_order(l_ref, 2 * grp + 1, dil, c))
        mx = jnp.maximum(jnp.maximum(ls[0], ls[1]), ls[2])
        es = [jnp.exp(l - mx) for l in ls]
        den = es[0] + es[1] + es[2]
        parts.append((es[0] * os_[0] + es[1] * os_[1] + es[2] * os_[2]) / den)
    a = jnp.concatenate(parts, axis=1).astype(BF16)
    b = b_ref[...]
    c = c_ref[...]
    h = h_ref[...]
    u = _rms_bf16(h, g_ref[...])
    D = D_MODEL
    ms = []
    for j in range(D // MERGE_N_TILE):
        cols = slice(j * MERGE_N_TILE, (j + 1) * MERGE_N_TILE)
        gates = [jax.nn.sigmoid(_dot(u, wgt_ref[:, k * D + j * MERGE_N_TILE:k * D + (j + 1) * MERGE_N_TILE]))
                 for k in range(3)]
        m = (gates[0] * _dot(a, wa_ref[:, cols]) + gates[1] * _dot(b, wb_ref[:, cols])
             + gates[2] * _dot(c, wc_ref[:, cols]))
        ms.append(m.astype(BF16))
    out_ref[...] = h + _dot(jnp.concatenate(ms, axis=1), wo_ref[...])


def _merge(h, g, outs, lses, b, c, w_gate, wa, wb, wc, wo, layer, S):
    T = h.shape[0]
    tm = TOKEN_TILE
    tiles = S // tm
    row = lambda w: pl.BlockSpec((tm, w), lambda i: (i, 0))
    sub = [pl.BlockSpec((None, dil, tm // dil, A_GROUP_WIDTH), lambda i: (i // tiles, 0, i % tiles, 0))
           for _, dil in A_GROUPS]
    return pl.pallas_call(
        _merge_kernel,
        out_shape=jax.ShapeDtypeStruct((T, D_MODEL), F32),
        grid=(T // tm,),
        in_specs=[row(D_MODEL), pl.BlockSpec((None, 1, D_MODEL), lambda i: (layer, 0, 0))] + sub + sub
                 + [row(B_WIDTH), row(C_WIDTH), _resident(D_MODEL, 3 * D_MODEL, layer),
                    _resident(A_GROUP_WIDTH, D_MODEL, layer), _resident(B_WIDTH, D_MODEL, layer),
                    _resident(C_WIDTH, D_MODEL, layer), _resident(D_MODEL, D_MODEL, layer)],
        out_specs=row(D_MODEL),
        scratch_shapes=[pltpu.VMEM((2 * len(A_GROUPS), A_GROUP_WIDTH // LANE, tm, LANE), F32)],
        compiler_params=_params("parallel"),
        name="merge",
    )(h, g, *outs, *lses, b, c, w_gate, wa, wb, wc, wo)


def _rope_tables(S):
    half = B_HEAD_DIM // 2
    inv = ROPE_BASE ** (-jnp.arange(half, dtype=F32) / half)
    ang = jnp.arange(S, dtype=F32)[:, None] * inv[None]
    cos, sin = jnp.cos(ang), jnp.sin(ang)
    return jnp.concatenate([cos, cos], axis=-1), jnp.concatenate([-sin, sin], axis=-1)


def _trunk(x, p, w):
    Bn, S, D = x.shape
    T = Bn * S
    h = x.reshape(T, D)
    p = p.reshape(DEPTH, T, D_PLE)
    cos2, sin2 = _rope_tables(S)
    ret_consts = _ret_consts()
    for l in range(DEPTH):
        h = _ffn(h, w["ffn1_norm"], w["ffn1_w_gate"], w["ffn1_w_up"], w["ffn1_w_down"], l)
        za0, za1, za2, zbc, zlog = _rms_proj(h, w["mix_norm"], w["w_in"], w["log_lb"][l][None],
                                             w["log_1mlb"][l][None], l, Bn, S)
        z3 = zbc.reshape(Bn, S, N_IN - 3 * A_WIDTH)
        outs, lses = _mixer_a((za0, za1, za2), w["rel_bias"])
        b = _mixer_b(z3, cos2, sin2, ret_consts, w["ret_norm"][l][None]).reshape(T, B_WIDTH)
        c = _mixer_c(z3, zlog.reshape(Bn, S, 2 * C_WIDTH), w["hgrn_norm"][l][None]).reshape(T, C_WIDTH)
        h = _merge(h, w["mix_norm"], outs, lses, b, c, w["w_merge_gate"],
                   w["w_branch_a"], w["w_branch_b"], w["w_branch_c"], w["w_out"], l, S)
        h = _ffn_ple(h, w["ffn2_norm"], w["ffn2_w_gate"], w["ffn2_w_up"], w["ffn2_w_down"],
                     p, w["ple_norm"], w["w_ple_gate"], w["w_ple_proj"], w["final_norm"], l, l == DEPTH - 1)
    return h.reshape(Bn, S, D)


def kernel(x_prompt, x_sample, p_prompt, p_sample, ffn1_norm, ffn1_w_gate, ffn1_w_up, ffn1_w_down, mix_norm, w_in, rel_bias, ret_norm, hgrn_lower_bound, hgrn_norm, w_branch_a, w_branch_b, w_branch_c, w_merge_gate, w_out, ffn2_norm, ffn2_w_gate, ffn2_w_up, ffn2_w_down, ple_norm, w_ple_gate, w_ple_proj, final_norm):
    lb = jax.nn.softmax(hgrn_lower_bound.astype(F32), axis=0)
    lb = jnp.cumsum(lb, axis=0)
    lbs = lb - lb[0]
    bf = lambda a: a.astype(BF16)
    vec = lambda a: a.astype(F32)[:, None, :]
    w = dict(
        ffn1_norm=vec(ffn1_norm), ffn1_w_gate=bf(ffn1_w_gate), ffn1_w_up=bf(ffn1_w_up), ffn1_w_down=bf(ffn1_w_down),
        mix_norm=vec(mix_norm), w_in=bf(w_in), rel_bias=rel_bias, ret_norm=ret_norm.astype(F32),
        hgrn_norm=hgrn_norm.astype(F32), log_lb=jnp.log(lbs), log_1mlb=jnp.log1p(-lbs),
        w_branch_a=bf(w_branch_a), w_branch_b=bf(w_branch_b), w_branch_c=bf(w_branch_c),
        w_merge_gate=bf(w_merge_gate), w_out=bf(w_out),
        ffn2_norm=vec(ffn2_norm), ffn2_w_gate=bf(ffn2_w_gate), ffn2_w_up=bf(ffn2_w_up), ffn2_w_down=bf(ffn2_w_down),
        ple_norm=vec(ple_norm), w_ple_gate=bf(w_ple_gate), w_ple_proj=bf(w_ple_proj),
        final_norm=final_norm.astype(F32)[None, :],
    )
    return (_trunk(x_prompt, p_prompt, w), _trunk(x_sample, p_sample, w))
```

```python
import functools
import math

import jax
import jax.numpy as jnp
import numpy as np
from jax import lax
from jax.experimental import pallas as pl
from jax.experimental.pallas import tpu as pltpu

F32 = jnp.float32
BF16 = jnp.bfloat16

EPS = 1e-6
D_MODEL = 1024
D_PLE = 256
D_FF = 2816
DEPTH = 4
A_GROUPS = ((128, 1), (512, 4), (2048, 16))
A_HEADS_PER_GROUP = 4
A_HEAD_DIM = 64
A_HEADS = 12
A_GROUP_WIDTH = A_HEADS_PER_GROUP * A_HEAD_DIM
A_WIDTH = 768
A_BLOCK = 64
A_SCALE = A_HEAD_DIM ** -0.5
assert math.log2(A_SCALE) == round(math.log2(A_SCALE))
A_BLOCKS_PER_STEP = 8
N_BUCKETS = 32
MAX_DISTANCE = 1024
B_HEADS = 4
B_HEAD_DIM = 128
B_WIDTH = 512
RET_TILE = 256
ROPE_BASE = 10000.0
C_HEADS = 4
C_HEAD_DIM = 128
C_WIDTH = 512
GLA_CHUNK = 64
GLA_TILE = 256
N_IN = 3 * A_WIDTH + 4 * B_WIDTH + 5 * C_WIDTH
assert A_GROUPS[0][1] == 1 and all(d > 1 for _, d in A_GROUPS[1:])
LANE = 128
COL_B = 0
COL_C = COL_B + 4 * B_WIDTH // LANE
HEAD_COLS = B_WIDTH // LANE
assert B_WIDTH == C_WIDTH
SLAB_BG, SLAB_CQ, SLAB_CF_FWD, SLAB_CF_BWD, SLAB_CG = 3, 4, 5, 6, 8

VMEM_LIMIT_BYTES = 56 * 1024 * 1024
TOKEN_TILE = 512
FFN_TOKEN_TILE = 1024
FFN_F_TILE = 256
PROJ_N_TILE = 256
MERGE_N_TILE = 256


def _params(*sem):
    return pltpu.CompilerParams(dimension_semantics=sem, vmem_limit_bytes=VMEM_LIMIT_BYTES)


def _rms_bf16(x, g):
    ms = jnp.mean(x * x, axis=-1, keepdims=True)
    return (x * lax.rsqrt(ms + EPS) * g).astype(BF16)


def _dot(a, b):
    return jnp.dot(a, b, preferred_element_type=F32)


def _dot_nt(a, b):
    return lax.dot_general(a, b, (((1,), (1,)), ((), ())), preferred_element_type=F32)


def _dot_tn(a, b):
    return lax.dot_general(a, b, (((0,), (0,)), ((), ())), preferred_element_type=F32)


def _swiglu_residual(x, g_ref, wg_ref, wu_ref, wd_ref):
    u = _rms_bf16(x, g_ref[...])
    acc = None
    for f in range(D_FF // FFN_F_TILE):
        cols = slice(f * FFN_F_TILE, (f + 1) * FFN_F_TILE)
        a = _dot(u, wg_ref[:, cols])
        b = _dot(u, wu_ref[:, cols])
        hid = (a * jax.nn.sigmoid(a) * b).astype(BF16)
        d = _dot(hid, wd_ref[cols, :])
        acc = d if acc is None else acc + d
    return x + 0.5 * acc


def _ffn_kernel(x_ref, g_ref, wg_ref, wu_ref, wd_ref, o_ref):
    o_ref[...] = _swiglu_residual(x_ref[...], g_ref, wg_ref, wu_ref, wd_ref)


def _ffn_ple_kernel(x_ref, g_ref, wg_ref, wu_ref, wd_ref, p_ref, pg_ref, wpg_ref, wpp_ref, fg_ref, o_ref, *, final):
    h = _swiglu_residual(x_ref[...], g_ref, wg_ref, wu_ref, wd_ref)
    gate = jax.nn.sigmoid(_dot(_rms_bf16(h, pg_ref[...]), wpg_ref[...]))
    h = h + gate * _dot(p_ref[...].astype(BF16), wpp_ref[...])
    if final:
        ms = jnp.mean(h * h, axis=-1, keepdims=True)
        h = h * lax.rsqrt(ms + EPS) * fg_ref[...]
    o_ref[...] = h


def _resident(rows, cols, layer):
    return pl.BlockSpec((None, rows, cols), lambda i: (layer, 0, 0), pipeline_mode=pl.Buffered(1))


def _ffn(h, g, wg, wu, wd, layer):
    T = h.shape[0]
    tm = FFN_TOKEN_TILE
    return pl.pallas_call(
        _ffn_kernel,
        out_shape=jax.ShapeDtypeStruct((T, D_MODEL), F32),
        grid=(T // tm,),
        in_specs=[
            pl.BlockSpec((tm, D_MODEL), lambda i: (i, 0)),
            pl.BlockSpec((None, 1, D_MODEL), lambda i: (layer, 0, 0)),
            _resident(D_MODEL, D_FF, layer), _resident(D_MODEL, D_FF, layer), _resident(D_FF, D_MODEL, layer),
        ],
        out_specs=pl.BlockSpec((tm, D_MODEL), lambda i: (i, 0)),
        compiler_params=_params("parallel"),
        name="ffn",
    )(h, g, wg, wu, wd)


def _ffn_ple(h, g, wg, wu, wd, p, pg, wpg, wpp, final_g, layer, final):
    T = h.shape[0]
    tm = FFN_TOKEN_TILE
    return pl.pallas_call(
        functools.partial(_ffn_ple_kernel, final=final),
        out_shape=jax.ShapeDtypeStruct((T, D_MODEL), F32),
        grid=(T // tm,),
        in_specs=[
            pl.BlockSpec((tm, D_MODEL), lambda i: (i, 0)),
            pl.BlockSpec((None, 1, D_MODEL), lambda i: (layer, 0, 0)),
            _resident(D_MODEL, D_FF, layer), _resident(D_MODEL, D_FF, layer), _resident(D_FF, D_MODEL, layer),
            pl.BlockSpec((None, tm, D_PLE), lambda i: (layer, i, 0)),
            pl.BlockSpec((None, 1, D_MODEL), lambda i: (layer, 0, 0)),
            _resident(D_MODEL, D_MODEL, layer), _resident(D_PLE, D_MODEL, layer),
            pl.BlockSpec((1, D_MODEL), lambda i: (0, 0)),
        ],
        out_specs=pl.BlockSpec((tm, D_MODEL), lambda i: (i, 0)),
        compiler_params=_params("parallel"),
        name="ffn_ple",
    )(h, g, wg, wu, wd, p, pg, wpg, wpp, final_g)


def _proj_kernel(x_ref, g_ref, win_ref, llb_ref, l1m_ref, za0_ref, za1_ref, za2_ref, zbc_ref, zlog_ref, stage):
    tn = PROJ_N_TILE
    tm = TOKEN_TILE
    W = A_GROUP_WIDTH
    za_refs = (za0_ref, za1_ref, za2_ref)
    u = _rms_bf16(x_ref[...], g_ref[...])

    light, medium, heavy = [], [], []

    def a_epilogue(j, grp, dil):
        def run(res):
            if j == 0:
                res = res * A_SCALE
            dst = za_refs[grp]
            if dil == 1:
                dst[0, :, j * W:(j + 1) * W] = res.astype(BF16)
                return
            halves = W // LANE
            for c in range(halves):
                stage[j % 2, grp - 1, c] = res[:, c * LANE:(c + 1) * LANE]
            for r in range(dil):
                for c in range(halves):
                    sub = stage[j % 2, grp - 1, c, pl.ds(r, tm // dil, stride=dil), :]
                    dst[r, :, j * W + c * LANE:j * W + (c + 1) * LANE] = sub.astype(BF16)
        return run

    for j in range(3):
        for grp, (_, dil) in enumerate(A_GROUPS):
            col0 = j * A_WIDTH + grp * W
            light.append((slice(col0, col0 + W), a_epilogue(j, grp, dil)))

    def bc_epilogue(j):
        slab, off = divmod(j * tn, B_WIDTH)

        def run(res):
            if slab in (SLAB_BG, SLAB_CQ, SLAB_CG):
                res = res * jax.nn.sigmoid(res)
            elif slab in (SLAB_CF_FWD, SLAB_CF_BWD):
                lcol = (slab - SLAB_CF_FWD) * C_WIDTH + off
                logf, res = _gate(res, llb_ref[:, off:off + tn], l1m_ref[:, off:off + tn])
                zlog_ref[:, lcol:lcol + tn] = logf
            zbc_ref[:, j * tn:(j + 1) * tn] = res.astype(BF16)

        kind = heavy if slab in (SLAB_CF_FWD, SLAB_CF_BWD) else medium if slab in (SLAB_BG, SLAB_CQ, SLAB_CG) else light
        return kind, run

    for j in range((N_IN - 3 * A_WIDTH) // tn):
        kind, run = bc_epilogue(j)
        kind.append((slice(3 * A_WIDTH + j * tn, 3 * A_WIDTH + (j + 1) * tn), run))

    order = []
    busy = heavy + medium
    per = -(-len(light) // max(len(busy), 1))
    for t in busy:
        order.append(t)
        order.extend(light[:per])
        light = light[per:]
    order.extend(light)

    for cols, run in order:
        run(_dot(u, win_ref[:, cols]))


def _rms_proj(h, g, w_in, log_lb, log_1mlb, layer, Bn, S):
    T = h.shape[0]
    tm = TOKEN_TILE
    tiles = S // tm
    n_bc = N_IN - 3 * A_WIDTH
    za_shapes, za_specs = [], []
    for _, dil in A_GROUPS:
        assert tm % (dil * 16) == 0 and S % tm == 0
        za_shapes.append(jax.ShapeDtypeStruct((Bn, dil, S // dil, A_WIDTH), BF16))
        za_specs.append(pl.BlockSpec((None, dil, tm // dil, A_WIDTH), lambda i: (i // tiles, 0, i % tiles, 0)))
    return pl.pallas_call(
        _proj_kernel,
        out_shape=(*za_shapes, jax.ShapeDtypeStruct((T, n_bc), BF16), jax.ShapeDtypeStruct((T, 2 * C_WIDTH), F32)),
        grid=(T // tm,),
        in_specs=[
            pl.BlockSpec((tm, D_MODEL), lambda i: (i, 0)),
            pl.BlockSpec((None, 1, D_MODEL), lambda i: (layer, 0, 0)),
            _resident(D_MODEL, N_IN, layer),
            pl.BlockSpec((1, C_WIDTH), lambda i: (0, 0)),
            pl.BlockSpec((1, C_WIDTH), lambda i: (0, 0)),
        ],
        out_specs=(*za_specs, pl.BlockSpec((tm, n_bc), lambda i: (i, 0)), pl.BlockSpec((tm, 2 * C_WIDTH), lambda i: (i, 0))),
        scratch_shapes=[pltpu.VMEM((2, len(A_GROUPS) - 1, A_GROUP_WIDTH // LANE, tm, LANE), F32)],
        compiler_params=_params("parallel"),
        name="proj_in",
    )(h, g, w_in, log_lb, log_1mlb)


def _t5_bucket_np(rel):
    half = N_BUCKETS // 2
    max_exact = half // 2
    ret = np.where(rel > 0, half, 0)
    n = np.abs(rel)
    nf = np.maximum(n, 1).astype(np.float32)
    large = max_exact + (
        np.log(nf / np.float32(max_exact)) / np.float32(math.log(MAX_DISTANCE / max_exact)) * np.float32(half - max_exact)
    ).astype(np.int32)
    large = np.minimum(large, half - 1)
    return ret + np.where(n < max_exact, n, large)


def _band_attn_kernel(q_ref, kin_ref, vin_ref, bias_ref, o_ref, lse_ref, k_ref, v_ref, *, seq_len, n_res):
    C = A_BLOCK
    H = A_HEADS_PER_GROUP
    hd = A_HEAD_DIM
    W = A_GROUP_WIDTH
    n_blocks = seq_len // C

    for src, dst in ((kin_ref, k_ref), (vin_ref, v_ref)):
        for r in range(n_res):
            dst[r, 0:C, :] = jnp.zeros((C, W), BF16)
            dst[r, C + seq_len:2 * C + seq_len, :] = jnp.zeros((C, W), BF16)
            dst[r, C:C + seq_len, :] = src[r]

    lane_head_q = lax.broadcasted_iota(jnp.int32, (C, W), 1) // hd

    def scores(r, n):
        base = pl.multiple_of(n * C, C)
        qb = q_ref[r, pl.ds(base, C), :]
        kb = k_ref[r, pl.ds(base, 3 * C), :]
        zero = jnp.zeros_like(qb)
        qexp = jnp.concatenate([jnp.where(lane_head_q == j, qb, zero) for j in range(H)], axis=0)
        return _dot_nt(qexp, kb)

    def softmax(n, s):
        edge = jnp.where(n == 0, 1, 0) + jnp.where(n == n_blocks - 1, 2, 0)
        s = s + bias_ref[edge]
        m = jnp.max(s, axis=-1, keepdims=True)
        e = jnp.exp(s - m)
        den = jnp.sum(e, axis=-1, keepdims=True)
        return (e / den).astype(BF16), m + jnp.log(den)

    def output(r, n, p, lse):
        base = pl.multiple_of(n * C, C)
        oall = _dot(p, v_ref[r, pl.ds(base, 3 * C), :])
        o = jnp.zeros((C, W), F32)
        lse_b = jnp.zeros((C, W), F32)
        for j in range(H):
            sel = lane_head_q == j
            o = jnp.where(sel, oall[j * C:(j + 1) * C, :], o)
            lse_b = jnp.where(sel, lse[j * C:(j + 1) * C, :], lse_b)
        o_ref[r, pl.ds(base, C), :] = o.astype(o_ref.dtype)
        lse_ref[r, pl.ds(base, C), :] = lse_b

    units = n_res * n_blocks
    nb = math.gcd(units, A_BLOCKS_PER_STEP)

    def body(i, carry):
        ts = [i * nb + t for t in range(nb)]
        rns = [(t // n_blocks, t % n_blocks) for t in ts]
        ss = [scores(r, n) for r, n in rns]
        pls = [softmax(n, s) for (_, n), s in zip(rns, ss)]
        for (r, n), (p, lse) in zip(rns, pls):
            output(r, n, p, lse)
        return carry

    lax.fori_loop(0, units // nb, body, 0)


def _band_attn(za, bias):
    Bn, dil, L, _ = za.shape
    C = A_BLOCK
    W = A_GROUP_WIDTH
    n_res = math.gcd(dil, max(1, A_BLOCKS_PER_STEP // (L // C)))
    steps = dil // n_res
    kern = functools.partial(_band_attn_kernel, seq_len=L, n_res=n_res)
    col = lambda c: pl.BlockSpec((None, n_res, L, W), lambda i: (i // steps, i % steps, 0, c))
    return pl.pallas_call(
        kern,
        out_shape=(jax.ShapeDtypeStruct((Bn, dil, L, W), BF16), jax.ShapeDtypeStruct((Bn, dil, L, W), F32)),
        grid=(Bn * steps,),
        in_specs=[col(0), col(1), col(2), pl.BlockSpec((4, A_HEADS_PER_GROUP * C, 3 * C), lambda i: (0, 0, 0))],
        out_specs=(col(0), col(0)),
        scratch_shapes=[pltpu.VMEM((n_res, L + 2 * C, W), BF16), pltpu.VMEM((n_res, L + 2 * C, W), BF16)],
        compiler_params=_params("parallel"),
        name="band_attn",
    )(za, za, za, bias)


def _mixer_a(zas, rel_bias):
    C = A_BLOCK
    outs, lses = [], []
    qi = np.arange(C)[:, None]
    kj = np.arange(3 * C)[None, :]
    off = kj - C - qi
    for g, (window, dil) in enumerate(A_GROUPS):
        n_side = (window // 2) // dil
        L = zas[g].shape[2]
        assert L % C == 0
        bucket = _t5_bucket_np((off * dil).astype(np.int32))
        tbl = rel_bias[:, g * A_HEADS_PER_GROUP:(g + 1) * A_HEADS_PER_GROUP].astype(F32)
        onehot = bucket[None] == np.arange(N_BUCKETS)[:, None, None]
        bias = jnp.sum(jnp.where(onehot[:, None], tbl[:, :, None, None], 0.0), axis=0)
        band = np.abs(off) <= n_side
        left, right = kj >= C, kj < 2 * C
        masks = np.stack([band, band & left, band & right, band & left & right])
        bias = jnp.where(masks[:, None], bias[None], -1e30).reshape(4, A_HEADS_PER_GROUP * C, 3 * C)
        o, lse = _band_attn(zas[g], bias)
        outs.append(o)
        lses.append(lse)
    return outs, lses


def _ret_consts():
    C = RET_TILE
    hh = jnp.arange(B_HEADS, dtype=F32)
    lgf = jnp.log1p(-jnp.exp2(-5.0 - hh))[:, None, None]
    lgb = jnp.log1p(-jnp.exp2(-5.5 - hh))[:, None, None]
    i = jnp.arange(C, dtype=F32)[None, :, None]
    j = jnp.arange(C, dtype=F32)[None, None, :]
    rel = i - j
    dfb = (jnp.where(rel >= 0, jnp.exp(jnp.maximum(rel, 0.0) * lgf), 0.0)
           + jnp.where(rel <= 0, jnp.exp(jnp.maximum(-rel, 0.0) * lgb), 0.0))
    ones = jnp.ones((1, 1, LANE), F32)
    rows = jnp.stack([jnp.exp((i + 1.0) * lgf) * ones,
                      jnp.exp((C - i) * lgb) * ones,
                      jnp.exp((C - 1.0 - i) * lgf) * ones,
                      jnp.exp(i * lgb) * ones], axis=1)
    chunk = jnp.concatenate([jnp.exp(C * lgf) * ones, jnp.exp(C * lgb) * ones], axis=1)
    return dfb, rows, chunk


def _ret_kernel(q_ref, k_ref, v_ref, g_ref, cos_ref, sin_ref, d_ref, r_ref, c_ref, gain_ref, o_ref,
                os_, qh_s, kvt_s, st_s, *, seq_len):
    C = RET_TILE
    n_chunks = seq_len // C
    dv = B_HEAD_DIM
    scale = B_HEAD_DIM ** -0.5

    def rot(x, sl):
        return x * cos_ref[sl, :] + pltpu.roll(x, B_HEAD_DIM // 2, 1) * sin_ref[sl, :]

    nb = math.gcd(n_chunks, 4)

    def local(i, carry):
        ns = [i * nb + t for t in range(nb)]
        sls = [pl.ds(pl.multiple_of(n * C, C), C) for n in ns]
        qs = [rot(q_ref[sl, :].astype(F32), sl) for sl in sls]
        ks = [rot(k_ref[sl, :].astype(F32), sl) * scale for sl in sls]
        scs = [_dot_nt(q.astype(BF16), k.astype(BF16)) for q, k in zip(qs, ks)]
        khats = [jnp.concatenate([(k * r_ref[2]).astype(BF16), (k * r_ref[3]).astype(BF16)], axis=1) for k in ks]
        for n, sl, khat in zip(ns, sls, khats):
            kvt_s[n] = _dot_tn(v_ref[sl, :], khat)
        for sl, q in zip(sls, qs):
            qh_s[sl, 0:dv] = (q * r_ref[0]).astype(BF16)
            qh_s[sl, dv:2 * dv] = (q * r_ref[1]).astype(BF16)
        for sl, s in zip(sls, scs):
            os_[sl, :] = _dot((s * d_ref[...]).astype(BF16), v_ref[sl, :])
        return carry

    lax.fori_loop(0, n_chunks // nb, local, 0)

    gcf = c_ref[0:1, :]
    gcb = c_ref[1:2, :]

    def scan(i, carry):
        sf, sb = carry
        nb = n_chunks - 1 - i
        st_s[i, :, 0:dv] = sf.astype(BF16)
        sf = sf * gcf + kvt_s[i, :, 0:dv]
        st_s[nb, :, dv:2 * dv] = sb.astype(BF16)
        sb = sb * gcb + kvt_s[nb, :, dv:2 * dv]
        return sf, sb

    zero = jnp.zeros((dv, dv), F32)
    lax.fori_loop(0, n_chunks, scan, (zero, zero))

    def finish(i, carry):
        ns = [i * nb + t for t in range(nb)]
        sls = [pl.ds(pl.multiple_of(n * C, C), C) for n in ns]
        outs = [os_[sl, :] + _dot_nt(qh_s[sl, :], st_s[n]) for n, sl in zip(ns, sls)]
        for sl, o in zip(sls, outs):
            mu = jnp.mean(o, axis=-1, keepdims=True)
            d = o - mu
            var = jnp.mean(d * d, axis=-1, keepdims=True)
            y = d * lax.rsqrt(var + EPS) * gain_ref[...]
            o_ref[sl, :] = (g_ref[sl, :].astype(F32) * y).astype(o_ref.dtype)
        return carry

    lax.fori_loop(0, n_chunks // nb, finish, 0)


def _mixer_b(z3, cos2, sin2, consts, gain):
    Bn, S, _ = z3.shape
    C = RET_TILE
    dfb, rows, chunk = consts

    def zspec(slab):
        return pl.BlockSpec((None, S, LANE), lambda b, h: (b, 0, COL_B + slab * HEAD_COLS + h))

    return pl.pallas_call(
        functools.partial(_ret_kernel, seq_len=S),
        out_shape=jax.ShapeDtypeStruct((Bn, S, B_WIDTH), BF16),
        grid=(Bn, B_HEADS),
        in_specs=[
            zspec(0), zspec(1), zspec(2), zspec(3),
            pl.BlockSpec((S, LANE), lambda b, h: (0, 0)),
            pl.BlockSpec((S, LANE), lambda b, h: (0, 0)),
            pl.BlockSpec((None, C, C), lambda b, h: (h, 0, 0)),
            pl.BlockSpec((None, 4, C, LANE), lambda b, h: (h, 0, 0, 0)),
            pl.BlockSpec((None, 2, LANE), lambda b, h: (h, 0, 0)),
            pl.BlockSpec((1, LANE), lambda b, h: (0, h)),
        ],
        out_specs=pl.BlockSpec((None, S, LANE), lambda b, h: (b, 0, h)),
        scratch_shapes=[
            pltpu.VMEM((S, LANE), F32),
            pltpu.VMEM((S, 2 * LANE), BF16),
            pltpu.VMEM((S // C, LANE, 2 * LANE), F32),
            pltpu.VMEM((S // C, LANE, 2 * LANE), BF16),
        ],
        compiler_params=_params("parallel", "arbitrary"),
        name="retention",
    )(z3, z3, z3, z3, cos2, sin2, dfb, rows, chunk, gain)


def _split3(x):
    hi = x.astype(BF16)
    r1 = x - hi.astype(F32)
    mid = r1.astype(BF16)
    lo = (r1 - mid.astype(F32)).astype(BF16)
    return hi, mid, lo


def _chunk_cumsums(tril, xf, xb, n_chunks, chunk):
    width = xf.shape[-1]
    parts = jnp.concatenate([*_split3(xf), *_split3(xb)], axis=1)
    cs = _dot(tril, parts)
    pf = cs[:, 0:width] + cs[:, width:2 * width] + cs[:, 2 * width:3 * width]
    pb = cs[:, 3 * width:4 * width] + cs[:, 4 * width:5 * width] + cs[:, 5 * width:6 * width]
    pb3 = pb.reshape(n_chunks, chunk, width)
    sb = pb3[:, chunk - 1:chunk, :] - pb3 + xb.reshape(n_chunks, chunk, width)
    return pf, sb.reshape(n_chunks * chunk, width)


def _gate(z, log_lb, log_1mlb):
    t = jnp.log(1.0 + jnp.exp(-jnp.abs(z)))
    ls_pos = jnp.minimum(z, 0.0) - t
    ls_neg = jnp.minimum(-z, 0.0) - t
    c = log_1mlb + ls_pos
    mx = jnp.maximum(log_lb, c)
    logf = mx + jnp.log(1.0 + jnp.exp(-jnp.abs(log_lb - c)))
    key = jnp.exp(log_1mlb + ls_neg)
    return logf, key


def _gla_masks():
    i = np.arange(GLA_TILE)
    same = (i[:, None] // GLA_CHUNK) == (i[None, :] // GLA_CHUNK)
    m = np.stack([same & (i[:, None] >= i[None, :]), same & (i[:, None] <= i[None, :])]).astype(np.float32)
    return jnp.asarray(m, F32), jnp.asarray(m[0], BF16)


def _gla_kernel(q_ref, kf_ref, kb_ref, v_ref, g_ref, lf_ref, lb_ref, gain_ref, mask_ref, tri_ref, o_ref,
                os_, qh_s, kvt_s, st_s, gf_s, gb_s, *, seq_len):
    C = GLA_CHUNK
    R = GLA_TILE
    NC = R // C
    n_tiles = seq_len // R
    n_chunks = seq_len // C
    dk = C_HEAD_DIM

    def scaled(q3, key, b, ref_row, last_row):
        b3 = b.reshape(NC, C, dk)
        key3 = key.reshape(NC, C, dk)
        b_ref = b3[:, ref_row:ref_row + 1, :]
        b_last = b3[:, last_row:last_row + 1, :]
        qt = (q3 * jnp.exp(b3 - b_ref)).reshape(R, dk).astype(BF16)
        kt = (key3 * jnp.exp(b_ref - b3)).reshape(R, dk).astype(BF16)
        qh = (q3 * jnp.exp(b3)).reshape(R, dk).astype(BF16)
        kh = (key3 * jnp.exp(b_last - b3)).reshape(R, dk).astype(BF16)
        return qt, kt, qh, kh, jnp.exp(b_last).reshape(NC, dk)

    nt = math.gcd(n_tiles, 4)

    def local(i, carry):
        ts = [i * nt + u for u in range(nt)]
        sls = [pl.ds(pl.multiple_of(t * R, R), R) for t in ts]
        sums = [_chunk_cumsums(tri_ref[...], lf_ref[sl, :], lb_ref[sl, :], NC, C) for sl in sls]
        bfs = [s[0] for s in sums]
        bbs = [s[1] for s in sums]
        q3s = [q_ref[sl, :].astype(F32).reshape(NC, C, dk) for sl in sls]
        fwd = [scaled(q3, kf_ref[sl, :].astype(F32), b, C // 2 - 1, C - 1) for q3, sl, b in zip(q3s, sls, bfs)]
        bwd = [scaled(q3, kb_ref[sl, :].astype(F32), b, C // 2, 0) for q3, sl, b in zip(q3s, sls, bbs)]
        sc_f = [_dot_nt(f[0], f[1]) for f in fwd]
        sc_b = [_dot_nt(b[0], b[1]) for b in bwd]
        for t, sl, f, b in zip(ts, sls, fwd, bwd):
            v = v_ref[sl, :]
            khat = jnp.concatenate([f[3], b[3]], axis=1)
            cbase = t * NC
            for c in range(NC):
                rows = slice(c * C, (c + 1) * C)
                kvt_s[cbase + c] = _dot_tn(v[rows, :], khat[rows, :])
            qh_s[sl, 0:dk] = f[2]
            qh_s[sl, dk:2 * dk] = b[2]
            gf_s[pl.ds(cbase, NC), :] = f[4]
            gb_s[pl.ds(cbase, NC), :] = b[4]
        for sl, sf, sb in zip(sls, sc_f, sc_b):
            att = jnp.where(mask_ref[0] > 0.5, sf, 0.0) + jnp.where(mask_ref[1] > 0.5, sb, 0.0)
            os_[sl, :] = _dot(att.astype(BF16), v_ref[sl, :])
        return carry

    lax.fori_loop(0, n_tiles // nt, local, 0)

    def scan(i, carry):
        sf, sb = carry
        nb = n_chunks - 1 - i
        st_s[i, :, 0:dk] = sf.astype(BF16)
        sf = sf * gf_s[pl.ds(i, 1), :] + kvt_s[i, :, 0:dk]
        st_s[nb, :, dk:2 * dk] = sb.astype(BF16)
        sb = sb * gb_s[pl.ds(nb, 1), :] + kvt_s[nb, :, dk:2 * dk]
        return sf, sb

    zero = jnp.zeros((dk, dk), F32)
    lax.fori_loop(0, n_chunks, scan, (zero, zero), unroll=2)

    nb = math.gcd(n_tiles, 4)

    def finish(i, carry):
        outs = []
        for t in [i * nb + u for u in range(nb)]:
            base = pl.multiple_of(t * R, R)
            cbase = t * NC
            parts = []
            for c in range(NC):
                rows = pl.ds(base + c * C, C)
                parts.append(os_[rows, :] + _dot_nt(qh_s[rows, :], st_s[cbase + c]))
            outs.append((pl.ds(base, R), jnp.concatenate(parts, axis=0)))
        for sl, o in outs:
            y = o * lax.rsqrt(jnp.mean(o * o, axis=-1, keepdims=True) + EPS) * gain_ref[...]
            o_ref[sl, :] = (g_ref[sl, :].astype(F32) * y).astype(o_ref.dtype)
        return carry

    lax.fori_loop(0, n_tiles // nb, finish, 0)


def _mixer_c(z3, zlog3, gain):
    Bn, S, _ = z3.shape
    n_chunks = S // GLA_CHUNK

    def zspec(slab):
        return pl.BlockSpec((None, S, LANE), lambda b, h: (b, 0, COL_C + slab * HEAD_COLS + h))

    def lspec(direction):
        return pl.BlockSpec((None, S, LANE), lambda b, h: (b, 0, direction * HEAD_COLS + h))

    vec = pl.BlockSpec((1, LANE), lambda b, h: (0, h))
    tile_const = pl.BlockSpec((2, GLA_TILE, GLA_TILE), lambda b, h: (0, 0, 0))
    mask_f32, mask_bf16 = _gla_masks()
    return pl.pallas_call(
        functools.partial(_gla_kernel, seq_len=S),
        out_shape=jax.ShapeDtypeStruct((Bn, S, C_WIDTH), BF16),
        grid=(Bn, C_HEADS),
        in_specs=[zspec(0), zspec(1), zspec(2), zspec(3), zspec(4), lspec(0), lspec(1), vec, tile_const,
                  pl.BlockSpec((GLA_TILE, GLA_TILE), lambda b, h: (0, 0))],
        out_specs=pl.BlockSpec((None, S, LANE), lambda b, h: (b, 0, h)),
        scratch_shapes=[
            pltpu.VMEM((S, LANE), F32),
            pltpu.VMEM((S, 2 * LANE), BF16),
            pltpu.VMEM((n_chunks, LANE, 2 * LANE), F32),
            pltpu.VMEM((n_chunks, LANE, 2 * LANE), BF16),
            pltpu.VMEM((n_chunks, LANE), F32),
            pltpu.VMEM((n_chunks, LANE), F32),
        ],
        compiler_params=_params("parallel", "arbitrary"),
        name="hgrn2",
    )(z3, z3, z3, z3, z3, zlog3, zlog3, gain, mask_f32, mask_bf16)


def _merge_kernel(h_ref, g_ref, o0, o1, o2, l0, l1, l2, b_ref, c_ref, wgt_ref, wa_ref, wb_ref, wc_ref, wo_ref,
                  out_ref, stage):
    tm = TOKEN_TILE
    halves = A_GROUP_WIDTH // LANE

    def token_order(ref, slot, dil, c):
        cols = slice(c * LANE, (c + 1) * LANE)
        if dil == 1:
            return ref[0, :, cols].astype(F32)
        for r in range(dil):
            stage[slot, c, pl.ds(r, tm // dil, stride=dil), :] = ref[r, :, cols].astype(F32)
        return stage[slot, c]

    parts = []
    for c in range(halves):
        os_, ls = [], []
        for grp, (o_ref, l_ref) in enumerate(((o0, l0), (o1, l1), (o2, l2))):
            dil = A_GROUPS[grp][1]
            os_.append(token_order(o_ref, 2 * grp, dil, c))
            ls.append(token_order(l_ref, 2 * grp + 1, dil, c))
        mx = jnp.maximum(jnp.maximum(ls[0], ls[1]), ls[2])
        es = [jnp.exp(l - mx) for l in ls]
        den = es[0] + es[1] + es[2]
        parts.append((es[0] * os_[0] + es[1] * os_[1] + es[2] * os_[2]) / den)
    a = jnp.concatenate(parts, axis=1).astype(BF16)
    b = b_ref[...]
    c = c_ref[...]
    h = h_ref[...]
    u = _rms_bf16(h, g_ref[...])
    D = D_MODEL
    ms = []
    for j in range(D // MERGE_N_TILE):
        cols = slice(j * MERGE_N_TILE, (j + 1) * MERGE_N_TILE)
        gates = [jax.nn.sigmoid(_dot(u, wgt_ref[:, k * D + j * MERGE_N_TILE:k * D + (j + 1) * MERGE_N_TILE]))
                 for k in range(3)]
        m = (gates[0] * _dot(a, wa_ref[:, cols]) + gates[1] * _dot(b, wb_ref[:, cols])
             + gates[2] * _dot(c, wc_ref[:, cols]))
        ms.append(m.astype(BF16))
    out_ref[...] = h + _dot(jnp.concatenate(ms, axis=1), wo_ref[...])


def _merge(h, g, outs, lses, b, c, w_gate, wa, wb, wc, wo, layer, S):
    T = h.shape[0]
    tm = TOKEN_TILE
    tiles = S // tm
    row = lambda w: pl.BlockSpec((tm, w), lambda i: (i, 0))
    sub = [pl.BlockSpec((None, dil, tm // dil, A_GROUP_WIDTH), lambda i: (i // tiles, 0, i % tiles, 0))
           for _, dil in A_GROUPS]
    return pl.pallas_call(
        _merge_kernel,
        out_shape=jax.ShapeDtypeStruct((T, D_MODEL), F32),
        grid=(T // tm,),
        in_specs=[row(D_MODEL), pl.BlockSpec((None, 1, D_MODEL), lambda i: (layer, 0, 0))] + sub + sub
                 + [row(B_WIDTH), row(C_WIDTH), _resident(D_MODEL, 3 * D_MODEL, layer),
                    _resident(A_GROUP_WIDTH, D_MODEL, layer), _resident(B_WIDTH, D_MODEL, layer),
                    _resident(C_WIDTH, D_MODEL, layer), _resident(D_MODEL, D_MODEL, layer)],
        out_specs=row(D_MODEL),
        scratch_shapes=[pltpu.VMEM((2 * len(A_GROUPS), A_GROUP_WIDTH // LANE, tm, LANE), F32)],
        compiler_params=_params("parallel"),
        name="merge",
    )(h, g, *outs, *lses, b, c, w_gate, wa, wb, wc, wo)


def _rope_tables(S):
    half = B_HEAD_DIM // 2
    inv = ROPE_BASE ** (-jnp.arange(half, dtype=F32) / half)
    ang = jnp.arange(S, dtype=F32)[:, None] * inv[None]
    cos, sin = jnp.cos(ang), jnp.sin(ang)
    return jnp.concatenate([cos, cos], axis=-1), jnp.concatenate([-sin, sin], axis=-1)


def _trunk(x, p, w):
    Bn, S, D = x.shape
    T = Bn * S
    h = x.reshape(T, D)
    p = p.reshape(DEPTH, T, D_PLE)
    cos2, sin2 = _rope_tables(S)
    ret_consts = _ret_consts()
    for l in range(DEPTH):
        h = _ffn(h, w["ffn1_norm"], w["ffn1_w_gate"], w["ffn1_w_up"], w["ffn1_w_down"], l)
        za0, za1, za2, zbc, zlog = _rms_proj(h, w["mix_norm"], w["w_in"], w["log_lb"][l][None],
                                             w["log_1mlb"][l][None], l, Bn, S)
        z3 = zbc.reshape(Bn, S, N_IN - 3 * A_WIDTH)
        outs, lses = _mixer_a((za0, za1, za2), w["rel_bias"])
        b = _mixer_b(z3, cos2, sin2, ret_consts, w["ret_norm"][l][None]).reshape(T, B_WIDTH)
        c = _mixer_c(z3, zlog.reshape(Bn, S, 2 * C_WIDTH), w["hgrn_norm"][l][None]).reshape(T, C_WIDTH)
        h = _merge(h, w["mix_norm"], outs, lses, b, c, w["w_merge_gate"],
                   w["w_branch_a"], w["w_branch_b"], w["w_branch_c"], w["w_out"], l, S)
        h = _ffn_ple(h, w["ffn2_norm"], w["ffn2_w_gate"], w["ffn2_w_up"], w["ffn2_w_down"],
                     p, w["ple_norm"], w["w_ple_gate"], w["w_ple_proj"], w["final_norm"], l, l == DEPTH - 1)
    return h.reshape(Bn, S, D)


def kernel(x_prompt, x_sample, p_prompt, p_sample, ffn1_norm, ffn1_w_gate, ffn1_w_up, ffn1_w_down, mix_norm, w_in, rel_bias, ret_norm, hgrn_lower_bound, hgrn_norm, w_branch_a, w_branch_b, w_branch_c, w_merge_gate, w_out, ffn2_norm, ffn2_w_gate, ffn2_w_up, ffn2_w_down, ple_norm, w_ple_gate, w_ple_proj, final_norm):
    lb = jax.nn.softmax(hgrn_lower_bound.astype(F32), axis=0)
    lb = jnp.cumsum(lb, axis=0)
    lbs = lb - lb[0]
    bf = lambda a: a.astype(BF16)
    vec = lambda a: a.astype(F32)[:, None, :]
    w = dict(
        ffn1_norm=vec(ffn1_norm), ffn1_w_gate=bf(ffn1_w_gate), ffn1_w_up=bf(ffn1_w_up), ffn1_w_down=bf(ffn1_w_down),
        mix_norm=vec(mix_norm), w_in=bf(w_in), rel_bias=rel_bias, ret_norm=ret_norm.astype(F32),
        hgrn_norm=hgrn_norm.astype(F32), log_lb=jnp.log(lbs), log_1mlb=jnp.log1p(-lbs),
        w_branch_a=bf(w_branch_a), w_branch_b=bf(w_branch_b), w_branch_c=bf(w_branch_c),
        w_merge_gate=bf(w_merge_gate), w_out=bf(w_out),
        ffn2_norm=vec(ffn2_norm), ffn2_w_gate=bf(ffn2_w_gate), ffn2_w_up=bf(ffn2_w_up), ffn2_w_down=bf(ffn2_w_down),
        ple_norm=vec(ple_norm), w_ple_gate=bf(w_ple_gate), w_ple_proj=bf(w_ple_proj),
        final_norm=final_norm.astype(F32)[None, :],
    )
    return (_trunk(x_prompt, p_prompt, w), _trunk(x_sample, p_sample, w))
```

```python
import functools
import math

import jax
import jax.numpy as jnp
import numpy as np
from jax import lax
from jax.experimental import pallas as pl
from jax.experimental.pallas import tpu as pltpu

F32 = jnp.float32
BF16 = jnp.bfloat16

EPS = 1e-6
D_MODEL = 1024
D_PLE = 256
D_FF = 2816
DEPTH = 4
A_GROUPS = ((128, 1), (512, 4), (2048, 16))
A_HEADS_PER_GROUP = 4
A_HEAD_DIM = 64
A_HEADS = 12
A_GROUP_WIDTH = A_HEADS_PER_GROUP * A_HEAD_DIM
A_WIDTH = 768
A_BLOCK = 64
A_SCALE = A_HEAD_DIM ** -0.5
assert math.log2(A_SCALE) == round(math.log2(A_SCALE))
A_BLOCKS_PER_STEP = 8
N_BUCKETS = 32
MAX_DISTANCE = 1024
B_HEADS = 4
B_HEAD_DIM = 128
B_WIDTH = 512
RET_TILE = 256
ROPE_BASE = 10000.0
C_HEADS = 4
C_HEAD_DIM = 128
C_WIDTH = 512
GLA_CHUNK = 64
GLA_TILE = 256
N_IN = 3 * A_WIDTH + 4 * B_WIDTH + 5 * C_WIDTH
assert A_GROUPS[0][1] == 1 and all(d > 1 for _, d in A_GROUPS[1:])
LANE = 128
COL_B = 0
COL_C = COL_B + 4 * B_WIDTH // LANE
HEAD_COLS = B_WIDTH // LANE
assert B_WIDTH == C_WIDTH
SLAB_BG, SLAB_CQ, SLAB_CF_FWD, SLAB_CF_BWD, SLAB_CG = 3, 4, 5, 6, 8

VMEM_LIMIT_BYTES = 56 * 1024 * 1024
TOKEN_TILE = 512
FFN_TOKEN_TILE = 1024
FFN_F_TILE = 256
PROJ_N_TILE = 256
MERGE_N_TILE = 256


def _params(*sem):
    return pltpu.CompilerParams(dimension_semantics=sem, vmem_limit_bytes=VMEM_LIMIT_BYTES)


def _rms_bf16(x, g):
    ms = jnp.mean(x * x, axis=-1, keepdims=True)
    return (x * lax.rsqrt(ms + EPS) * g).astype(BF16)


def _dot(a, b):
    return jnp.dot(a, b, preferred_element_type=F32)


def _dot_nt(a, b):
    return lax.dot_general(a, b, (((1,), (1,)), ((), ())), preferred_element_type=F32)


def _dot_tn(a, b):
    return lax.dot_general(a, b, (((0,), (0,)), ((), ())), preferred_element_type=F32)


def _swiglu_residual(x, g_ref, wg_ref, wu_ref, wd_ref):
    u = _rms_bf16(x, g_ref[...])
    acc = None
    for f in range(D_FF // FFN_F_TILE):
        cols = slice(f * FFN_F_TILE, (f + 1) * FFN_F_TILE)
        a = _dot(u, wg_ref[:, cols])
        b = _dot(u, wu_ref[:, cols])
        hid = (a * jax.nn.sigmoid(a) * b).astype(BF16)
        d = _dot(hid, wd_ref[cols, :])
        acc = d if acc is None else acc + d
    return x + 0.5 * acc


def _ffn_kernel(x_ref, g_ref, wg_ref, wu_ref, wd_ref, o_ref):
    o_ref[...] = _swiglu_residual(x_ref[...], g_ref, wg_ref, wu_ref, wd_ref)


def _ffn_ple_kernel(x_ref, g_ref, wg_ref, wu_ref, wd_ref, p_ref, pg_ref, wpg_ref, wpp_ref, fg_ref, o_ref, *, final):
    h = _swiglu_residual(x_ref[...], g_ref, wg_ref, wu_ref, wd_ref)
    gate = jax.nn.sigmoid(_dot(_rms_bf16(h, pg_ref[...]), wpg_ref[...]))
    h = h + gate * _dot(p_ref[...].astype(BF16), wpp_ref[...])
    if final:
        ms = jnp.mean(h * h, axis=-1, keepdims=True)
        h = h * lax.rsqrt(ms + EPS) * fg_ref[...]
    o_ref[...] = h


def _resident(rows, cols, layer):
    return pl.BlockSpec((None, rows, cols), lambda i: (layer, 0, 0), pipeline_mode=pl.Buffered(1))


def _ffn(h, g, wg, wu, wd, layer):
    T = h.shape[0]
    tm = FFN_TOKEN_TILE
    return pl.pallas_call(
        _ffn_kernel,
        out_shape=jax.ShapeDtypeStruct((T, D_MODEL), F32),
        grid=(T // tm,),
        in_specs=[
            pl.BlockSpec((tm, D_MODEL), lambda i: (i, 0)),
            pl.BlockSpec((None, 1, D_MODEL), lambda i: (layer, 0, 0)),
            _resident(D_MODEL, D_FF, layer), _resident(D_MODEL, D_FF, layer), _resident(D_FF, D_MODEL, layer),
        ],
        out_specs=pl.BlockSpec((tm, D_MODEL), lambda i: (i, 0)),
        compiler_params=_params("parallel"),
        name="ffn",
    )(h, g, wg, wu, wd)


def _ffn_ple(h, g, wg, wu, wd, p, pg, wpg, wpp, final_g, layer, final):
    T = h.shape[0]
    tm = FFN_TOKEN_TILE
    return pl.pallas_call(
        functools.partial(_ffn_ple_kernel, final=final),
        out_shape=jax.ShapeDtypeStruct((T, D_MODEL), F32),
        grid=(T // tm,),
        in_specs=[
            pl.BlockSpec((tm, D_MODEL), lambda i: (i, 0)),
            pl.BlockSpec((None, 1, D_MODEL), lambda i: (layer, 0, 0)),
            _resident(D_MODEL, D_FF, layer), _resident(D_MODEL, D_FF, layer), _resident(D_FF, D_MODEL, layer),
            pl.BlockSpec((None, tm, D_PLE), lambda i: (layer, i, 0)),
            pl.BlockSpec((None, 1, D_MODEL), lambda i: (layer, 0, 0)),
            _resident(D_MODEL, D_MODEL, layer), _resident(D_PLE, D_MODEL, layer),
            pl.BlockSpec((1, D_MODEL), lambda i: (0, 0)),
        ],
        out_specs=pl.BlockSpec((tm, D_MODEL), lambda i: (i, 0)),
        compiler_params=_params("parallel"),
        name="ffn_ple",
    )(h, g, wg, wu, wd, p, pg, wpg, wpp, final_g)


def _proj_kernel(x_ref, g_ref, win_ref, llb_ref, l1m_ref, za0_ref, za1_ref, za2_ref, zbc_ref, zlog_ref, stage):
    tn = PROJ_N_TILE
    tm = TOKEN_TILE
    W = A_GROUP_WIDTH
    za_refs = (za0_ref, za1_ref, za2_ref)
    u = _rms_bf16(x_ref[...], g_ref[...])

    light, medium, heavy = [], [], []

    def a_epilogue(j, grp, dil):
        def run(res):
            if j == 0:
                res = res * A_SCALE
            dst = za_refs[grp]
            if dil == 1:
                dst[0, :, j * W:(j + 1) * W] = res.astype(BF16)
                return
            halves = W // LANE
            for c in range(halves):
                stage[j % 2, grp - 1, c] = res[:, c * LANE:(c + 1) * LANE]
            for r in range(dil):
                for c in range(halves):
                    sub = stage[j % 2, grp - 1, c, pl.ds(r, tm // dil, stride=dil), :]
                    dst[r, :, j * W + c * LANE:j * W + (c + 1) * LANE] = sub.astype(BF16)
        return run

    for j in range(3):
        for grp, (_, dil) in enumerate(A_GROUPS):
            col0 = j * A_WIDTH + grp * W
            light.append((slice(col0, col0 + W), a_epilogue(j, grp, dil)))

    def bc_epilogue(j):
        slab, off = divmod(j * tn, B_WIDTH)

        def run(res):
            if slab in (SLAB_BG, SLAB_CQ, SLAB_CG):
                res = res * jax.nn.sigmoid(res)
            elif slab in (SLAB_CF_FWD, SLAB_CF_BWD):
                lcol = (slab - SLAB_CF_FWD) * C_WIDTH + off
                logf, res = _gate(res, llb_ref[:, off:off + tn], l1m_ref[:, off:off + tn])
                zlog_ref[:, lcol:lcol + tn] = logf
            zbc_ref[:, j * tn:(j + 1) * tn] = res.astype(BF16)

        kind = heavy if slab in (SLAB_CF_FWD, SLAB_CF_BWD) else medium if slab in (SLAB_BG, SLAB_CQ, SLAB_CG) else light
        return kind, run

    for j in range((N_IN - 3 * A_WIDTH) // tn):
        kind, run = bc_epilogue(j)
        kind.append((slice(3 * A_WIDTH + j * tn, 3 * A_WIDTH + (j + 1) * tn), run))

    order = []
    busy = heavy + medium
    per = -(-len(light) // max(len(busy), 1))
    for t in busy:
        order.append(t)
        order.extend(light[:per])
        light = light[per:]
    order.extend(light)

    for cols, run in order:
        run(_dot(u, win_ref[:, cols]))


def _rms_proj(h, g, w_in, log_lb, log_1mlb, layer, Bn, S):
    T = h.shape[0]
    tm = TOKEN_TILE
    tiles = S // tm
    n_bc = N_IN - 3 * A_WIDTH
    za_shapes, za_specs = [], []
    for _, dil in A_GROUPS:
        assert tm % (dil * 16) == 0 and S % tm == 0
        za_shapes.append(jax.ShapeDtypeStruct((Bn, dil, S // dil, A_WIDTH), BF16))
        za_specs.append(pl.BlockSpec((None, dil, tm // dil, A_WIDTH), lambda i: (i // tiles, 0, i % tiles, 0)))
    return pl.pallas_call(
        _proj_kernel,
        out_shape=(*za_shapes, jax.ShapeDtypeStruct((T, n_bc), BF16), jax.ShapeDtypeStruct((T, 2 * C_WIDTH), F32)),
        grid=(T // tm,),
        in_specs=[
            pl.BlockSpec((tm, D_MODEL), lambda i: (i, 0)),
            pl.BlockSpec((None, 1, D_MODEL), lambda i: (layer, 0, 0)),
            _resident(D_MODEL, N_IN, layer),
            pl.BlockSpec((1, C_WIDTH), lambda i: (0, 0)),
            pl.BlockSpec((1, C_WIDTH), lambda i: (0, 0)),
        ],
        out_specs=(*za_specs, pl.BlockSpec((tm, n_bc), lambda i: (i, 0)), pl.BlockSpec((tm, 2 * C_WIDTH), lambda i: (i, 0))),
        scratch_shapes=[pltpu.VMEM((2, len(A_GROUPS) - 1, A_GROUP_WIDTH // LANE, tm, LANE), F32)],
        compiler_params=_params("parallel"),
        name="proj_in",
    )(h, g, w_in, log_lb, log_1mlb)


def _t5_bucket_np(rel):
    half = N_BUCKETS // 2
    max_exact = half // 2
    ret = np.where(rel > 0, half, 0)
    n = np.abs(rel)
    nf = np.maximum(n, 1).astype(np.float32)
    large = max_exact + (
        np.log(nf / np.float32(max_exact)) / np.float32(math.log(MAX_DISTANCE / max_exact)) * np.float32(half - max_exact)
    ).astype(np.int32)
    large = np.minimum(large, half - 1)
    return ret + np.where(n < max_exact, n, large)


def _band_attn_kernel(q_ref, kin_ref, vin_ref, bias_ref, o_ref, lse_ref, k_ref, v_ref, *, seq_len, n_res):
    C = A_BLOCK
    H = A_HEADS_PER_GROUP
    hd = A_HEAD_DIM
    W = A_GROUP_WIDTH
    n_blocks = seq_len // C

    for src, dst in ((kin_ref, k_ref), (vin_ref, v_ref)):
        for r in range(n_res):
            dst[r, 0:C, :] = jnp.zeros((C, W), BF16)
            dst[r, C + seq_len:2 * C + seq_len, :] = jnp.zeros((C, W), BF16)
            dst[r, C:C + seq_len, :] = src[r]

    lane_head_q = lax.broadcasted_iota(jnp.int32, (C, W), 1) // hd

    def scores(r, n):
        base = pl.multiple_of(n * C, C)
        qb = q_ref[r, pl.ds(base, C), :]
        kb = k_ref[r, pl.ds(base, 3 * C), :]
        zero = jnp.zeros_like(qb)
        qexp = jnp.concatenate([jnp.where(lane_head_q == j, qb, zero) for j in range(H)], axis=0)
        return _dot_nt(qexp, kb)

    def softmax(n, s):
        edge = jnp.where(n == 0, 1, 0) + jnp.where(n == n_blocks - 1, 2, 0)
        s = s + bias_ref[edge]
        m = jnp.max(s, axis=-1, keepdims=True)
        e = jnp.exp(s - m)
        den = jnp.sum(e, axis=-1, keepdims=True)
        return (e / den).astype(BF16), m + jnp.log(den)

    def output(r, n, p, lse):
        base = pl.multiple_of(n * C, C)
        oall = _dot(p, v_ref[r, pl.ds(base, 3 * C), :])
        o = jnp.zeros((C, W), F32)
        lse_b = jnp.zeros((C, W), F32)
        for j in range(H):
            sel = lane_head_q == j
            o = jnp.where(sel, oall[j * C:(j + 1) * C, :], o)
            lse_b = jnp.where(sel, lse[j * C:(j + 1) * C, :], lse_b)
        o_ref[r, pl.ds(base, C), :] = o.astype(o_ref.dtype)
        lse_ref[r, pl.ds(base, C), :] = lse_b

    units = n_res * n_blocks
    nb = math.gcd(units, A_BLOCKS_PER_STEP)

    def body(i, carry):
        ts = [i * nb + t for t in range(nb)]
        rns = [(t // n_blocks, t % n_blocks) for t in ts]
        ss = [scores(r, n) for r, n in rns]
        pls = [softmax(n, s) for (_, n), s in zip(rns, ss)]
        for (r, n), (p, lse) in zip(rns, pls):
            output(r, n, p, lse)
        return carry

    lax.fori_loop(0, units // nb, body, 0)


def _band_attn(za, bias):
    Bn, dil, L, _ = za.shape
    C = A_BLOCK
    W = A_GROUP_WIDTH
    n_res = math.gcd(dil, max(1, A_BLOCKS_PER_STEP // (L // C)))
    steps = dil // n_res
    kern = functools.partial(_band_attn_kernel, seq_len=L, n_res=n_res)
    col = lambda c: pl.BlockSpec((None, n_res, L, W), lambda i: (i // steps, i % steps, 0, c))
    return pl.pallas_call(
        kern,
        out_shape=(jax.ShapeDtypeStruct((Bn, dil, L, W), BF16), jax.ShapeDtypeStruct((Bn, dil, L, W), F32)),
        grid=(Bn * steps,),
        in_specs=[col(0), col(1), col(2), pl.BlockSpec((4, A_HEADS_PER_GROUP * C, 3 * C), lambda i: (0, 0, 0))],
        out_specs=(col(0), col(0)),
        scratch_shapes=[pltpu.VMEM((n_res, L + 2 * C, W), BF16), pltpu.VMEM((n_res, L + 2 * C, W), BF16)],
        compiler_params=_params("parallel"),
        name="band_attn",
    )(za, za, za, bias)


def _mixer_a(zas, rel_bias):
    C = A_BLOCK
    outs, lses = [], []
    qi = np.arange(C)[:, None]
    kj = np.arange(3 * C)[None, :]
    off = kj - C - qi
    for g, (window, dil) in enumerate(A_GROUPS):
        n_side = (window // 2) // dil
        L = zas[g].shape[2]
        assert L % C == 0
        bucket = _t5_bucket_np((off * dil).astype(np.int32))
        tbl = rel_bias[:, g * A_HEADS_PER_GROUP:(g + 1) * A_HEADS_PER_GROUP].astype(F32)
        onehot = bucket[None] == np.arange(N_BUCKETS)[:, None, None]
        bias = jnp.sum(jnp.where(onehot[:, None], tbl[:, :, None, None], 0.0), axis=0)
        band = np.abs(off) <= n_side
        left, right = kj >= C, kj < 2 * C
        masks = np.stack([band, band & left, band & right, band & left & right])
        bias = jnp.where(masks[:, None], bias[None], -1e30).reshape(4, A_HEADS_PER_GROUP * C, 3 * C)
        o, lse = _band_attn(zas[g], bias)
        outs.append(o)
        lses.append(lse)
    return outs, lses


def _ret_consts():
    C = RET_TILE
    hh = jnp.arange(B_HEADS, dtype=F32)
    lgf = jnp.log1p(-jnp.exp2(-5.0 - hh))[:, None, None]
    lgb = jnp.log1p(-jnp.exp2(-5.5 - hh))[:, None, None]
    i = jnp.arange(C, dtype=F32)[None, :, None]
    j = jnp.arange(C, dtype=F32)[None, None, :]
    rel = i - j
    dfb = (jnp.where(rel >= 0, jnp.exp(jnp.maximum(rel, 0.0) * lgf), 0.0)
           + jnp.where(rel <= 0, jnp.exp(jnp.maximum(-rel, 0.0) * lgb), 0.0))
    ones = jnp.ones((1, 1, LANE), F32)
    rows = jnp.stack([jnp.exp((i + 1.0) * lgf) * ones,
                      jnp.exp((C - i) * lgb) * ones,
                      jnp.exp((C - 1.0 - i) * lgf) * ones,
                      jnp.exp(i * lgb) * ones], axis=1)
    chunk = jnp.concatenate([jnp.exp(C * lgf) * ones, jnp.exp(C * lgb) * ones], axis=1)
    return dfb, rows, chunk


def _ret_kernel(q_ref, k_ref, v_ref, g_ref, cos_ref, sin_ref, d_ref, r_ref, c_ref, gain_ref, o_ref,
                os_, qh_s, kvt_s, st_s, *, seq_len):
    C = RET_TILE
    n_chunks = seq_len // C
    dv = B_HEAD_DIM
    scale = B_HEAD_DIM ** -0.5

    def rot(x, sl):
        return x * cos_ref[sl, :] + pltpu.roll(x, B_HEAD_DIM // 2, 1) * sin_ref[sl, :]

    nb = math.gcd(n_chunks, 8)

    def local(i, carry):
        ns = [i * nb + t for t in range(nb)]
        sls = [pl.ds(pl.multiple_of(n * C, C), C) for n in ns]
        qs = [rot(q_ref[sl, :].astype(F32), sl) for sl in sls]
        ks = [rot(k_ref[sl, :].astype(F32), sl) * scale for sl in sls]
        scs = [_dot_nt(q.astype(BF16), k.astype(BF16)) for q, k in zip(qs, ks)]
        khats = [jnp.concatenate([(k * r_ref[2]).astype(BF16), (k * r_ref[3]).astype(BF16)], axis=1) for k in ks]
        for n, sl, khat in zip(ns, sls, khats):
            kvt_s[n] = _dot_tn(v_ref[sl, :], khat)
        for sl, q in zip(sls, qs):
            qh_s[sl, 0:dv] = (q * r_ref[0]).astype(BF16)
            qh_s[sl, dv:2 * dv] = (q * r_ref[1]).astype(BF16)
        for sl, s in zip(sls, scs):
            os_[sl, :] = _dot((s * d_ref[...]).astype(BF16), v_ref[sl, :])
        return carry

    lax.fori_loop(0, n_chunks // nb, local, 0)

    gcf = c_ref[0:1, :]
    gcb = c_ref[1:2, :]

    def scan(i, carry):
        sf, sb = carry
        nb = n_chunks - 1 - i
        st_s[i, :, 0:dv] = sf.astype(BF16)
        sf = sf * gcf + kvt_s[i, :, 0:dv]
        st_s[nb, :, dv:2 * dv] = sb.astype(BF16)
        sb = sb * gcb + kvt_s[nb, :, dv:2 * dv]
        return sf, sb

    zero = jnp.zeros((dv, dv), F32)
    lax.fori_loop(0, n_chunks, scan, (zero, zero))

    def finish(i, carry):
        ns = [i * nb + t for t in range(nb)]
        sls = [pl.ds(pl.multiple_of(n * C, C), C) for n in ns]
        outs = [os_[sl, :] + _dot_nt(qh_s[sl, :], st_s[n]) for n, sl in zip(ns, sls)]
        for sl, o in zip(sls, outs):
            mu = jnp.mean(o, axis=-1, keepdims=True)
            d = o - mu
            var = jnp.mean(d * d, axis=-1, keepdims=True)
            y = d * lax.rsqrt(var + EPS) * gain_ref[...]
            o_ref[sl, :] = (g_ref[sl, :].astype(F32) * y).astype(o_ref.dtype)
        return carry

    lax.fori_loop(0, n_chunks // nb, finish, 0)


def _mixer_b(z3, cos2, sin2, consts, gain):
    Bn, S, _ = z3.shape
    C = RET_TILE
    dfb, rows, chunk = consts

    def zspec(slab):
        return pl.BlockSpec((None, S, LANE), lambda b, h: (b, 0, COL_B + slab * HEAD_COLS + h))

    return pl.pallas_call(
        functools.partial(_ret_kernel, seq_len=S),
        out_shape=jax.ShapeDtypeStruct((Bn, S, B_WIDTH), BF16),
        grid=(Bn, B_HEADS),
        in_specs=[
            zspec(0), zspec(1), zspec(2), zspec(3),
            pl.BlockSpec((S, LANE), lambda b, h: (0, 0)),
            pl.BlockSpec((S, LANE), lambda b, h: (0, 0)),
            pl.BlockSpec((None, C, C), lambda b, h: (h, 0, 0)),
            pl.BlockSpec((None, 4, C, LANE), lambda b, h: (h, 0, 0, 0)),
            pl.BlockSpec((None, 2, LANE), lambda b, h: (h, 0, 0)),
            pl.BlockSpec((1, LANE), lambda b, h: (0, h)),
        ],
        out_specs=pl.BlockSpec((None, S, LANE), lambda b, h: (b, 0, h)),
        scratch_shapes=[
            pltpu.VMEM((S, LANE), F32),
            pltpu.VMEM((S, 2 * LANE), BF16),
            pltpu.VMEM((S // C, LANE, 2 * LANE), F32),
            pltpu.VMEM((S // C, LANE, 2 * LANE), BF16),
        ],
        compiler_params=_params("parallel", "arbitrary"),
        name="retention",
    )(z3, z3, z3, z3, cos2, sin2, dfb, rows, chunk, gain)


def _split3(x):
    hi = x.astype(BF16)
    r1 = x - hi.astype(F32)
    mid = r1.astype(BF16)
    lo = (r1 - mid.astype(F32)).astype(BF16)
    return hi, mid, lo


def _chunk_cumsums(tril, xf, xb, n_chunks, chunk):
    width = xf.shape[-1]
    parts = jnp.concatenate([*_split3(xf), *_split3(xb)], axis=1)
    cs = _dot(tril, parts)
    pf = cs[:, 0:width] + cs[:, width:2 * width] + cs[:, 2 * width:3 * width]
    pb = cs[:, 3 * width:4 * width] + cs[:, 4 * width:5 * width] + cs[:, 5 * width:6 * width]
    pb3 = pb.reshape(n_chunks, chunk, width)
    sb = pb3[:, chunk - 1:chunk, :] - pb3 + xb.reshape(n_chunks, chunk, width)
    return pf, sb.reshape(n_chunks * chunk, width)


def _gate(z, log_lb, log_1mlb):
    t = jnp.log(1.0 + jnp.exp(-jnp.abs(z)))
    ls_pos = jnp.minimum(z, 0.0) - t
    ls_neg = jnp.minimum(-z, 0.0) - t
    c = log_1mlb + ls_pos
    mx = jnp.maximum(log_lb, c)
    logf = mx + jnp.log(1.0 + jnp.exp(-jnp.abs(log_lb - c)))
    key = jnp.exp(log_1mlb + ls_neg)
    return logf, key


def _gla_masks():
    i = np.arange(GLA_TILE)
    same = (i[:, None] // GLA_CHUNK) == (i[None, :] // GLA_CHUNK)
    m = np.stack([same & (i[:, None] >= i[None, :]), same & (i[:, None] <= i[None, :])]).astype(np.float32)
    return jnp.asarray(m, F32), jnp.asarray(m[0], BF16)


def _gla_kernel(q_ref, kf_ref, kb_ref, v_ref, g_ref, lf_ref, lb_ref, gain_ref, mask_ref, tri_ref, o_ref,
                os_, qh_s, kvt_s, st_s, gf_s, gb_s, *, seq_len):
    C = GLA_CHUNK
    R = GLA_TILE
    NC = R // C
    n_tiles = seq_len // R
    n_chunks = seq_len // C
    dk = C_HEAD_DIM

    def scaled(q3, key, b, ref_row, last_row):
        b3 = b.reshape(NC, C, dk)
        key3 = key.reshape(NC, C, dk)
        b_ref = b3[:, ref_row:ref_row + 1, :]
        b_last = b3[:, last_row:last_row + 1, :]
        qt = (q3 * jnp.exp(b3 - b_ref)).reshape(R, dk).astype(BF16)
        kt = (key3 * jnp.exp(b_ref - b3)).reshape(R, dk).astype(BF16)
        qh = (q3 * jnp.exp(b3)).reshape(R, dk).astype(BF16)
        kh = (key3 * jnp.exp(b_last - b3)).reshape(R, dk).astype(BF16)
        return qt, kt, qh, kh, jnp.exp(b_last).reshape(NC, dk)

    nt = math.gcd(n_tiles, 4)

    def local(i, carry):
        ts = [i * nt + u for u in range(nt)]
        sls = [pl.ds(pl.multiple_of(t * R, R), R) for t in ts]
        sums = [_chunk_cumsums(tri_ref[...], lf_ref[sl, :], lb_ref[sl, :], NC, C) for sl in sls]
        bfs = [s[0] for s in sums]
        bbs = [s[1] for s in sums]
        q3s = [q_ref[sl, :].astype(F32).reshape(NC, C, dk) for sl in sls]
        fwd = [scaled(q3, kf_ref[sl, :].astype(F32), b, C // 2 - 1, C - 1) for q3, sl, b in zip(q3s, sls, bfs)]
        bwd = [scaled(q3, kb_ref[sl, :].astype(F32), b, C // 2, 0) for q3, sl, b in zip(q3s, sls, bbs)]
        sc_f = [_dot_nt(f[0], f[1]) for f in fwd]
        sc_b = [_dot_nt(b[0], b[1]) for b in bwd]
        for t, sl, f, b in zip(ts, sls, fwd, bwd):
            v = v_ref[sl, :]
            khat = jnp.concatenate([f[3], b[3]], axis=1)
            cbase = t * NC
            for c in range(NC):
                rows = slice(c * C, (c + 1) * C)
                kvt_s[cbase + c] = _dot_tn(v[rows, :], khat[rows, :])
            qh_s[sl, 0:dk] = f[2]
            qh_s[sl, dk:2 * dk] = b[2]
            gf_s[pl.ds(cbase, NC), :] = f[4]
            gb_s[pl.ds(cbase, NC), :] = b[4]
        for sl, sf, sb in zip(sls, sc_f, sc_b):
            att = jnp.where(mask_ref[0] > 0.5, sf, 0.0) + jnp.where(mask_ref[1] > 0.5, sb, 0.0)
            os_[sl, :] = _dot(att.astype(BF16), v_ref[sl, :])
        return carry

    lax.fori_loop(0, n_tiles // nt, local, 0)

    def scan(i, carry):
        sf, sb = carry
        nb = n_chunks - 1 - i
        st_s[i, :, 0:dk] = sf.astype(BF16)
        sf = sf * gf_s[pl.ds(i, 1), :] + kvt_s[i, :, 0:dk]
        st_s[nb, :, dk:2 * dk] = sb.astype(BF16)
        sb = sb * gb_s[pl.ds(nb, 1), :] + kvt_s[nb, :, dk:2 * dk]
        return sf, sb

    zero = jnp.zeros((dk, dk), F32)
    lax.fori_loop(0, n_chunks, scan, (zero, zero), unroll=2)

    nb = math.gcd(n_tiles, 8)

    def finish(i, carry):
        outs = []
        for t in [i * nb + u for u in range(nb)]:
            base = pl.multiple_of(t * R, R)
            cbase = t * NC
            parts = []
            for c in range(NC):
                rows = pl.ds(base + c * C, C)
                parts.append(os_[rows, :] + _dot_nt(qh_s[rows, :], st_s[cbase + c]))
            outs.append((pl.ds(base, R), jnp.concatenate(parts, axis=0)))
        for sl, o in outs:
            y = o * lax.rsqrt(jnp.mean(o * o, axis=-1, keepdims=True) + EPS) * gain_ref[...]
            o_ref[sl, :] = (g_ref[sl, :].astype(F32) * y).astype(o_ref.dtype)
        return carry

    lax.fori_loop(0, n_tiles // nb, finish, 0)


def _mixer_c(z3, zlog3, gain):
    Bn, S, _ = z3.shape
    n_chunks = S // GLA_CHUNK

    def zspec(slab):
        return pl.BlockSpec((None, S, LANE), lambda b, h: (b, 0, COL_C + slab * HEAD_COLS + h))

    def lspec(direction):
        return pl.BlockSpec((None, S, LANE), lambda b, h: (b, 0, direction * HEAD_COLS + h))

    vec = pl.BlockSpec((1, LANE), lambda b, h: (0, h))
    tile_const = pl.BlockSpec((2, GLA_TILE, GLA_TILE), lambda b, h: (0, 0, 0))
    mask_f32, mask_bf16 = _gla_masks()
    return pl.pallas_call(
        functools.partial(_gla_kernel, seq_len=S),
        out_shape=jax.ShapeDtypeStruct((Bn, S, C_WIDTH), BF16),
        grid=(Bn, C_HEADS),
        in_specs=[zspec(0), zspec(1), zspec(2), zspec(3), zspec(4), lspec(0), lspec(1), vec, tile_const,
                  pl.BlockSpec((GLA_TILE, GLA_TILE), lambda b, h: (0, 0))],
        out_specs=pl.BlockSpec((None, S, LANE), lambda b, h: (b, 0, h)),
        scratch_shapes=[
            pltpu.VMEM((S, LANE), F32),
            pltpu.VMEM((S, 2 * LANE), BF16),
            pltpu.VMEM((n_chunks, LANE, 2 * LANE), F32),
            pltpu.VMEM((n_chunks, LANE, 2 * LANE), BF16),
            pltpu.VMEM((n_chunks, LANE), F32),
            pltpu.VMEM((n_chunks, LANE), F32),
        ],
        compiler_params=_params("parallel", "arbitrary"),
        name="hgrn2",
    )(z3, z3, z3, z3, z3, zlog3, zlog3, gain, mask_f32, mask_bf16)


def _merge_kernel(h_ref, g_ref, o0, o1, o2, l0, l1, l2, b_ref, c_ref, wgt_ref, wa_ref, wb_ref, wc_ref, wo_ref,
                  out_ref, stage):
    tm = TOKEN_TILE
    halves = A_GROUP_WIDTH // LANE

    def token_order(ref, slot, dil, c):
        cols = slice(c * LANE, (c + 1) * LANE)
        if dil == 1:
            return ref[0, :, cols].astype(F32)
        for r in range(dil):
            stage[slot, c, pl.ds(r, tm // dil, stride=dil), :] = ref[r, :, cols].astype(F32)
        return stage[slot, c]

    parts = []
    for c in range(halves):
        os_, ls = [], []
        for grp, (o_ref, l_ref) in enumerate(((o0, l0), (o1, l1), (o2, l2))):
            dil = A_GROUPS[grp][1]
            os_.append(token_order(o_ref, 2 * grp, dil, c))
            ls.append(token_order(l_ref, 2 * grp + 1, dil, c))
        mx = jnp.maximum(jnp.maximum(ls[0], ls[1]), ls[2])
        es = [jnp.exp(l - mx) for l in ls]
        den = es[0] + es[1] + es[2]
        parts.append((es[0] * os_[0] + es[1] * os_[1] + es[2] * os_[2]) / den)
    a = jnp.concatenate(parts, axis=1).astype(BF16)
    b = b_ref[...]
    c = c_ref[...]
    h = h_ref[...]
    u = _rms_bf16(h, g_ref[...])
    D = D_MODEL
    ms = []
    for j in range(D // MERGE_N_TILE):
        cols = slice(j * MERGE_N_TILE, (j + 1) * MERGE_N_TILE)
        gates = [jax.nn.sigmoid(_dot(u, wgt_ref[:, k * D + j * MERGE_N_TILE:k * D + (j + 1) * MERGE_N_TILE]))
                 for k in range(3)]
        m = (gates[0] * _dot(a, wa_ref[:, cols]) + gates[1] * _dot(b, wb_ref[:, cols])
             + gates[2] * _dot(c, wc_ref[:, cols]))
        ms.append(m.astype(BF16))
    out_ref[...] = h + _dot(jnp.concatenate(ms, axis=1), wo_ref[...])


def _merge(h, g, outs, lses, b, c, w_gate, wa, wb, wc, wo, layer, S):
    T = h.shape[0]
    tm = TOKEN_TILE
    tiles = S // tm
    row = lambda w: pl.BlockSpec((tm, w), lambda i: (i, 0))
    sub = [pl.BlockSpec((None, dil, tm // dil, A_GROUP_WIDTH), lambda i: (i // tiles, 0, i % tiles, 0))
           for _, dil in A_GROUPS]
    return pl.pallas_call(
        _merge_kernel,
        out_shape=jax.ShapeDtypeStruct((T, D_MODEL), F32),
        grid=(T // tm,),
        in_specs=[row(D_MODEL), pl.BlockSpec((None, 1, D_MODEL), lambda i: (layer, 0, 0))] + sub + sub
                 + [row(B_WIDTH), row(C_WIDTH), _resident(D_MODEL, 3 * D_MODEL, layer),
                    _resident(A_GROUP_WIDTH, D_MODEL, layer), _resident(B_WIDTH, D_MODEL, layer),
                    _resident(C_WIDTH, D_MODEL, layer), _resident(D_MODEL, D_MODEL, layer)],
        out_specs=row(D_MODEL),
        scratch_shapes=[pltpu.VMEM((2 * len(A_GROUPS), A_GROUP_WIDTH // LANE, tm, LANE), F32)],
        compiler_params=_params("parallel"),
        name="merge",
    )(h, g, *outs, *lses, b, c, w_gate, wa, wb, wc, wo)


def _rope_tables(S):
    half = B_HEAD_DIM // 2
    inv = ROPE_BASE ** (-jnp.arange(half, dtype=F32) / half)
    ang = jnp.arange(S, dtype=F32)[:, None] * inv[None]
    cos, sin = jnp.cos(ang), jnp.sin(ang)
    return jnp.concatenate([cos, cos], axis=-1), jnp.concatenate([-sin, sin], axis=-1)


def _trunk(x, p, w):
    Bn, S, D = x.shape
    T = Bn * S
    h = x.reshape(T, D)
    p = p.reshape(DEPTH, T, D_PLE)
    cos2, sin2 = _rope_tables(S)
    ret_consts = _ret_consts()
    for l in range(DEPTH):
        h = _ffn(h, w["ffn1_norm"], w["ffn1_w_gate"], w["ffn1_w_up"], w["ffn1_w_down"], l)
        za0, za1, za2, zbc, zlog = _rms_proj(h, w["mix_norm"], w["w_in"], w["log_lb"][l][None],
                                             w["log_1mlb"][l][None], l, Bn, S)
        z3 = zbc.reshape(Bn, S, N_IN - 3 * A_WIDTH)
        outs, lses = _mixer_a((za0, za1, za2), w["rel_bias"])
        b = _mixer_b(z3, cos2, sin2, ret_consts, w["ret_norm"][l][None]).reshape(T, B_WIDTH)
        c = _mixer_c(z3, zlog.reshape(Bn, S, 2 * C_WIDTH), w["hgrn_norm"][l][None]).reshape(T, C_WIDTH)
        h = _merge(h, w["mix_norm"], outs, lses, b, c, w["w_merge_gate"],
                   w["w_branch_a"], w["w_branch_b"], w["w_branch_c"], w["w_out"], l, S)
        h = _ffn_ple(h, w["ffn2_norm"], w["ffn2_w_gate"], w["ffn2_w_up"], w["ffn2_w_down"],
                     p, w["ple_norm"], w["w_ple_gate"], w["w_ple_proj"], w["final_norm"], l, l == DEPTH - 1)
    return h.reshape(Bn, S, D)


def kernel(x_prompt, x_sample, p_prompt, p_sample, ffn1_norm, ffn1_w_gate, ffn1_w_up, ffn1_w_down, mix_norm, w_in, rel_bias, ret_norm, hgrn_lower_bound, hgrn_norm, w_branch_a, w_branch_b, w_branch_c, w_merge_gate, w_out, ffn2_norm, ffn2_w_gate, ffn2_w_up, ffn2_w_down, ple_norm, w_ple_gate, w_ple_proj, final_norm):
    lb = jax.nn.softmax(hgrn_lower_bound.astype(F32), axis=0)
    lb = jnp.cumsum(lb, axis=0)
    lbs = lb - lb[0]
    bf = lambda a: a.astype(BF16)
    vec = lambda a: a.astype(F32)[:, None, :]
    w = dict(
        ffn1_norm=vec(ffn1_norm), ffn1_w_gate=bf(ffn1_w_gate), ffn1_w_up=bf(ffn1_w_up), ffn1_w_down=bf(ffn1_w_down),
        mix_norm=vec(mix_norm), w_in=bf(w_in), rel_bias=rel_bias, ret_norm=ret_norm.astype(F32),
        hgrn_norm=hgrn_norm.astype(F32), log_lb=jnp.log(lbs), log_1mlb=jnp.log1p(-lbs),
        w_branch_a=bf(w_branch_a), w_branch_b=bf(w_branch_b), w_branch_c=bf(w_branch_c),
        w_merge_gate=bf(w_merge_gate), w_out=bf(w_out),
        ffn2_norm=vec(ffn2_norm), ffn2_w_gate=bf(ffn2_w_gate), ffn2_w_up=bf(ffn2_w_up), ffn2_w_down=bf(ffn2_w_down),
        ple_norm=vec(ple_norm), w_ple_gate=bf(w_ple_gate), w_ple_proj=bf(w_ple_proj),
        final_norm=final_norm.astype(F32)[None, :],
    )
    return (_trunk(x_prompt, p_prompt, w), _trunk(x_sample, p_sample, w))
```
